```python
import jax, jax.numpy as jnp
from jax import lax
import numpy as np

D_MODEL = 1024
BATCH = 4
SEQ = 8192
DEPTH = 2

HEAD_DIM = 64
ROT_DIM = HEAD_DIM // 4
ROPE_THETA = 500000.0
NORM_EPS = 1e-5

A_HEADS = 8
A_KV_HEADS = 2
WINDOW = 128
B_HEADS = 4
B_KEY_DIM = 128
B_VAL_DIM = 128
B_CHUNK = 64
C_HEADS = 16
C_KV_HEADS = 4
C_Q_RANK = 256
IDX_HEADS = 8
IDX_DIM = 64
IDX_ROT_DIM = IDX_DIM // 4
IDX_TOPK = 256
C_QBLOCK = 128
N_EXPERTS = 32
TOP_K = 4
D_FF = D_MODEL
SWIGLU_LIMIT = 7.0
SWIGLU_ALPHA = 1.702

N_EVEN = (DEPTH + 1) // 2
N_ODD = DEPTH // 2

AB_WIDTHS = (A_HEADS * HEAD_DIM, A_KV_HEADS * HEAD_DIM, A_KV_HEADS * HEAD_DIM,
             B_HEADS * B_KEY_DIM, B_HEADS * B_KEY_DIM, B_HEADS * B_VAL_DIM, B_HEADS * B_VAL_DIM)
AB_IN = sum(AB_WIDTHS)
AB_MIX = A_HEADS * HEAD_DIM + B_HEADS * B_VAL_DIM
C_WIDTHS = (C_Q_RANK, C_KV_HEADS * HEAD_DIM, C_KV_HEADS * HEAD_DIM, IDX_DIM, IDX_HEADS)
C_IN = sum(C_WIDTHS)
C_MIX = C_HEADS * HEAD_DIM

kernel_name = 'hybrid_swa_hgrn2_dsa_moe_block'

f32 = jnp.float32


def split_cols(t, widths):
    out, start = [], 0
    for w in widths:
        out.append(t[..., start:start + w])
        start += w
    return out


def rms_norm(t, w):
    tf = t.astype(f32)
    y = tf * lax.rsqrt(jnp.mean(tf * tf, axis=-1, keepdims=True) + NORM_EPS)
    return (y * w.astype(f32)).astype(t.dtype)


def layer_norm(t, w, b):
    tf = t.astype(f32)
    mu = jnp.mean(tf, axis=-1, keepdims=True)
    var = jnp.mean(jnp.square(tf - mu), axis=-1, keepdims=True)
    return ((tf - mu) * lax.rsqrt(var + NORM_EPS) * w.astype(f32) + b.astype(f32)).astype(t.dtype)


def rotary_tables(positions, rot_dim):
    inv_freq = ROPE_THETA ** (-jnp.arange(0, rot_dim, 2, dtype=f32) / rot_dim)
    ang = positions.astype(f32)[..., None] * inv_freq
    return jnp.cos(ang), jnp.sin(ang)


def apply_partial_rotary(t, cos, sin):
    half = cos.shape[-1]
    c = cos[:, :, None, :].astype(t.dtype)
    s = sin[:, :, None, :].astype(t.dtype)
    t1, t2, rest = t[..., :half], t[..., half:2 * half], t[..., 2 * half:]
    return jnp.concatenate([t1 * c - t2 * s, t2 * c + t1 * s, rest], axis=-1)


def sliding_window_sink_attention(q, k, v, sinks):
    B, S, HQ, dh = q.shape
    HKV = k.shape[2]
    G = HQ // HKV
    nb = S // WINDOW
    qb = q.reshape(B, nb, WINDOW, HKV, G, dh)

    def band(t):
        tb = t.reshape(B, nb, WINDOW, HKV, dh)
        prev = jnp.pad(tb, ((0, 0), (1, 0), (0, 0), (0, 0), (0, 0)))[:, :-1]
        return jnp.concatenate([prev, tb], axis=2)

    kw, vw = band(k), band(v)
    logits = jnp.einsum('bnqhgd,bnkhd->bnhgqk', qb, kw).astype(f32) * (dh ** -0.5)
    dist = (jnp.arange(WINDOW)[:, None] + WINDOW) - jnp.arange(2 * WINDOW)[None, :]
    in_window = (dist >= 0) & (dist < WINDOW)
    has_prev = (jnp.arange(nb)[:, None] > 0) | (jnp.arange(2 * WINDOW)[None, :] >= WINDOW)
    mask = in_window[None] & has_prev[:, None, :]
    logits = jnp.where(mask[None, :, None, None], logits, -jnp.inf)
    sink = sinks.astype(f32).reshape(HKV, G)[None, None, :, :, None, None]
    m = jnp.maximum(jnp.max(logits, axis=-1, keepdims=True), sink)
    p = jnp.exp(logits - m)
    p = p / (jnp.sum(p, axis=-1, keepdims=True) + jnp.exp(sink - m))
    out = jnp.einsum('bnhgqk,bnkhd->bnqhgd', p.astype(v.dtype), vw)
    return out.reshape(B, S, HQ, dh)


def hgrn2_chunkwise(q, f_logit, i, lb):
    B, S, H, K = q.shape
    V = i.shape[-1]
    nc = S // B_CHUNK
    f = lb + (1.0 - lb) * jax.nn.sigmoid(f_logit.astype(f32))
    log_f = jnp.log(f)
    k_in = 1.0 - f
    qf = jax.nn.silu(q.astype(f32)) * (K ** -0.5)

    def to_chunks(t):
        return t.reshape(B, nc, B_CHUNK, H, t.shape[-1]).transpose(1, 0, 3, 2, 4)

    causal = jnp.tril(jnp.ones((B_CHUNK, B_CHUNK), dtype=bool))

    def step(state, inp):
        qc, kc, vc, gc = inp
        b = jnp.cumsum(gc, axis=2)
        inter = jnp.einsum('bhck,bhkv->bhcv', qc * jnp.exp(b), state)
        diff = b[:, :, :, None, :] - b[:, :, None, :, :]
        decay = jnp.exp(jnp.where(causal[:, :, None], diff, -jnp.inf))
        scores = jnp.einsum('bhtk,bhsk,bhtsk->bhts', qc, kc, decay)
        intra = jnp.einsum('bhts,bhsv->bhtv', scores, vc)
        b_last = b[:, :, -1:, :]
        new_state = (jnp.exp(b_last[:, :, 0, :, None]) * state
                     + jnp.einsum('bhck,bhcv->bhkv', kc * jnp.exp(b_last - b), vc))
        return new_state, inter + intra

    state0 = jnp.zeros((B, H, K, V), f32)
    _, out = lax.scan(step, state0, (to_chunks(qf), to_chunks(k_in), to_chunks(i.astype(f32)), to_chunks(log_f)))
    return out.transpose(1, 0, 3, 2, 4).reshape(B, S, H, V)


def dsa_sparse_attention(q, k, v, q_idx, k_idx, w_idx):
    B, S, HQ, dh = q.shape
    HKV = k.shape[2]
    G = HQ // HKV
    n_sel = min(IDX_TOPK, S // 4)
    nb = S // C_QBLOCK

    def blocks(t):
        return jnp.moveaxis(t.reshape((B, nb, C_QBLOCK) + t.shape[2:]), 1, 0)

    k_idx32 = k_idx.astype(f32)
    key_pos = jnp.arange(S)
    gather = jax.vmap(lambda t, idx: t[idx])

    def one_block(args):
        n, qb, qib, wb = args
        q_pos = n * C_QBLOCK + jnp.arange(C_QBLOCK)
        idx_logits = jnp.einsum('bqhd,bsd->bqhs', qib.astype(f32), k_idx32) * (IDX_DIM ** -0.5)
        index_score = jnp.einsum('bqh,bqhs->bqs', wb.astype(f32), jax.nn.relu(idx_logits))
        causal = key_pos[None, :] <= q_pos[:, None]
        index_score = jnp.where(causal[None], index_score, -jnp.inf)
        _, sel = lax.top_k(index_score, n_sel)
        valid = sel <= q_pos[None, :, None]
        kg = gather(k, sel)
        vg = gather(v, sel)
        logits = jnp.einsum('bqhgd,bqkhd->bqhgk', qb.reshape(B, C_QBLOCK, HKV, G, dh), kg).astype(f32) * (dh ** -0.5)
        logits = jnp.where(valid[:, :, None, None, :], logits, -jnp.inf)
        p = jax.nn.softmax(logits, axis=-1)
        o = jnp.einsum('bqhgk,bqkhd->bqhgd', p.astype(v.dtype), vg)
        return o.reshape(B, C_QBLOCK, HQ, dh)

    out = lax.map(one_block, (jnp.arange(nb), blocks(q), blocks(q_idx), blocks(w_idx)))
    return jnp.moveaxis(out, 0, 1).reshape(B, S, HQ, dh)


def mixer_swa_hgrn2(h, rope, w_in, w_out, sinks, lb, o_norm_w):
    B, S, _ = h.shape
    cos, sin = rope
    a_q, a_k, a_v, b_q, b_f, b_i, b_g = split_cols(h @ w_in, AB_WIDTHS)
    a_q = apply_partial_rotary(a_q.reshape(B, S, A_HEADS, HEAD_DIM), cos, sin)
    a_k = apply_partial_rotary(a_k.reshape(B, S, A_KV_HEADS, HEAD_DIM), cos, sin)
    a_v = a_v.reshape(B, S, A_KV_HEADS, HEAD_DIM)
    out_a = sliding_window_sink_attention(a_q, a_k, a_v, sinks)
    o_b = hgrn2_chunkwise(b_q.reshape(B, S, B_HEADS, B_KEY_DIM), b_f.reshape(B, S, B_HEADS, B_KEY_DIM),
                          b_i.reshape(B, S, B_HEADS, B_VAL_DIM), lb.reshape(B_HEADS, B_KEY_DIM)).astype(h.dtype)
    out_b = rms_norm(o_b, o_norm_w) * jax.nn.silu(b_g.reshape(B, S, B_HEADS, B_VAL_DIM))
    mixed = jnp.concatenate([out_a.reshape(B, S, -1), out_b.reshape(B, S, -1)], axis=-1)
    return mixed @ w_out


def mixer_dsa(h, rope, rope_idx, w_in, q_norm_w, w_uq, w_iq, ik_norm_w, ik_norm_b, w_out):
    B, S, _ = h.shape
    cos, sin = rope
    cos_i, sin_i = rope_idx
    cq, k, v, ik, iw = split_cols(h @ w_in, C_WIDTHS)
    cq = rms_norm(cq, q_norm_w)
    q = apply_partial_rotary((cq @ w_uq).reshape(B, S, C_HEADS, HEAD_DIM), cos, sin)
    k = apply_partial_rotary(k.reshape(B, S, C_KV_HEADS, HEAD_DIM), cos, sin)
    v = v.reshape(B, S, C_KV_HEADS, HEAD_DIM)
    q_idx = apply_partial_rotary((cq @ w_iq).reshape(B, S, IDX_HEADS, IDX_DIM), cos_i, sin_i)
    k_idx = apply_partial_rotary(layer_norm(ik, ik_norm_w, ik_norm_b)[:, :, None, :], cos_i, sin_i)[:, :, 0, :]
    w_idx = iw * (IDX_HEADS ** -0.5)
    o = dsa_sparse_attention(q, k, v, q_idx, k_idx, w_idx)
    return o.reshape(B, S, C_MIX) @ w_out


def routed_expert_ffn(h, router_w, router_b, w1, b1, w2, b2):
    B, S, D = h.shape
    t = h.reshape(B * S, D)
    logits = (t @ router_w + router_b).astype(f32)
    top_v, top_i = lax.top_k(logits, TOP_K)
    gates = jax.nn.softmax(top_v, axis=-1)
    dense_gate = jnp.sum(jax.nn.one_hot(top_i, N_EXPERTS, dtype=f32) * gates[..., None], axis=1)
    out = jnp.zeros((B * S, D), f32)
    for e in range(N_EXPERTS):
        gu = t @ w1[e] + b1[e]
        g, u = gu[:, :D_FF], gu[:, D_FF:]
        g = jnp.minimum(g, SWIGLU_LIMIT)
        u = jnp.clip(u, -SWIGLU_LIMIT, SWIGLU_LIMIT)
        act = g * jax.nn.sigmoid(SWIGLU_ALPHA * g) * (u + 1.0)
        y = act @ w2[e] + b2[e]
        out = out + dense_gate[:, e:e + 1] * y.astype(f32)
    return out.astype(h.dtype).reshape(B, S, D)


def setup_inputs(seed: int = 0) -> dict:
    key = jax.random.key(seed)
    ks = iter(jax.random.split(key, 32))

    def nrm(shape, scale):
        return jax.random.normal(next(ks), shape, f32) * scale

    D = D_MODEL
    x = nrm((BATCH, SEQ, D), 1.0)
    c = nrm((BATCH, D), 1.0)
    offsets = jax.random.randint(next(ks), (BATCH, 1), 0, 4096, dtype=jnp.int32)
    positions = offsets + jnp.arange(SEQ, dtype=jnp.int32)[None, :]
    return {
        'x': x,
        'c': c,
        'positions': positions,
        'mod_w': nrm((DEPTH, D, 6 * D), 0.5 * D ** -0.5),
        'mod_b': nrm((DEPTH, 6 * D), 0.02),
        'norm_mix_w': 1.0 + nrm((DEPTH, D), 0.02),
        'norm_ffn_w': 1.0 + nrm((DEPTH, D), 0.02),
        'ab_w_in': nrm((N_EVEN, D, AB_IN), D ** -0.5),
        'ab_w_out': nrm((N_EVEN, AB_MIX, D), AB_MIX ** -0.5),
        'a_sinks': nrm((N_EVEN, A_HEADS), 0.5),
        'b_lb_logits': nrm((N_EVEN + 1, B_HEADS * B_KEY_DIM), 0.5),
        'b_onorm_w': 1.0 + nrm((N_EVEN, B_VAL_DIM), 0.02),
        'c_w_in': nrm((N_ODD, D, C_IN), D ** -0.5),
        'c_q_norm_w': 1.0 + nrm((N_ODD, C_Q_RANK), 0.02),
        'c_w_uq': nrm((N_ODD, C_Q_RANK, C_MIX), C_Q_RANK ** -0.5),
        'c_w_iq': nrm((N_ODD, C_Q_RANK, IDX_HEADS * IDX_DIM), C_Q_RANK ** -0.5),
        'c_ik_norm_w': 1.0 + nrm((N_ODD, IDX_DIM), 0.02),
        'c_ik_norm_b': nrm((N_ODD, IDX_DIM), 0.02),
        'c_w_out': nrm((N_ODD, C_MIX, D), C_MIX ** -0.5),
        'router_w': nrm((DEPTH, D, N_EXPERTS), D ** -0.5),
        'router_b': nrm((DEPTH, N_EXPERTS), 0.01),
        'moe_w1': nrm((DEPTH, N_EXPERTS, D, 2 * D_FF), D ** -0.5),
        'moe_b1': nrm((DEPTH, N_EXPERTS, 2 * D_FF), 0.02),
        'moe_w2': nrm((DEPTH, N_EXPERTS, D_FF, D), D_FF ** -0.5),
        'moe_b2': nrm((DEPTH, N_EXPERTS, D), 0.02),
        'final_norm_w': 1.0 + nrm((D,), 0.02),
    }


def reference(x, c, positions, mod_w, mod_b, norm_mix_w, norm_ffn_w, ab_w_in, ab_w_out, a_sinks,
              b_lb_logits, b_onorm_w, c_w_in, c_q_norm_w, c_w_uq, c_w_iq, c_ik_norm_w, c_ik_norm_b,
              c_w_out, router_w, router_b, moe_w1, moe_b1, moe_w2, moe_b2, final_norm_w):
    rope = rotary_tables(positions, ROT_DIM)
    rope_idx = rotary_tables(positions, IDX_ROT_DIM)
    lower_bounds = jnp.cumsum(jax.nn.softmax(b_lb_logits.astype(f32), axis=0), axis=0)
    cond = jax.nn.silu(c)
    for layer in range(DEPTH):
        mod = cond @ mod_w[layer] + mod_b[layer]
        sh1, sc1, g1, sh2, sc2, g2 = [m[:, None, :] for m in jnp.split(mod, 6, axis=-1)]
        h = rms_norm(x, norm_mix_w[layer]) * (1.0 + sc1) + sh1
        j = layer // 2
        if layer % 2 == 0:
            mix = mixer_swa_hgrn2(h, rope, ab_w_in[j], ab_w_out[j], a_sinks[j], lower_bounds[j], b_onorm_w[j])
        else:
            mix = mixer_dsa(h, rope, rope_idx, c_w_in[j], c_q_norm_w[j], c_w_uq[j], c_w_iq[j],
                            c_ik_norm_w[j], c_ik_norm_b[j], c_w_out[j])
        x = x + g1 * mix
        h = rms_norm(x, norm_ffn_w[layer]) * (1.0 + sc2) + sh2
        x = x + g2 * routed_expert_ffn(h, router_w[layer], router_b[layer], moe_w1[layer], moe_b1[layer],
                                       moe_w2[layer], moe_b2[layer])
    return rms_norm(x, final_norm_w)
```

```python
import functools

import jax
import jax.numpy as jnp
from jax import lax
from jax.experimental import pallas as pl
from jax.experimental.pallas import tpu as pltpu

f32 = jnp.float32
bf16 = jnp.bfloat16
i32 = jnp.int32

D_MODEL = 1024
HEAD_DIM = 64
ROT_DIM = HEAD_DIM // 4
ROPE_THETA = 500000.0
NORM_EPS = 1e-5
A_HEADS = 8
A_KV_HEADS = 2
WINDOW = 128
B_HEADS = 4
B_KEY_DIM = 128
B_VAL_DIM = 128
B_CHUNK = 64
C_HEADS = 16
C_KV_HEADS = 4
C_Q_RANK = 256
IDX_HEADS = 8
IDX_DIM = 64
IDX_ROT_DIM = IDX_DIM // 4
IDX_TOPK = 256
C_QBLOCK = 128
N_EXPERTS = 32
TOP_K = 4
D_FF = D_MODEL
SWIGLU_LIMIT = 7.0
SWIGLU_ALPHA = 1.702

LANES = 128
AB_IN = 2560
AB_Q0, AB_K0, AB_V0 = 0, 512, 640
AB_BQ0, AB_BF0, AB_BI0, AB_BG0 = 768, 1280, 1792, 2304
C_IN_PAD = 896
C_K0, C_V0, C_IK0 = 256, 512, 768

VMEM_LIMIT = 56 * 1024 * 1024
EXPERT_TILE = 256
KEY_CHUNK = 512
NEG_BIG = -1e30
INT_MIN = -(2 ** 31)

_NT = (((1,), (1,)), ((), ()))
_TN = (((0,), (0,)), ((), ()))


def _params(sem, vmem=VMEM_LIMIT):
    return pltpu.CompilerParams(dimension_semantics=sem, vmem_limit_bytes=vmem)


def _rot(t, c, s1, s2):
    return t * c + pltpu.roll(t, LANES - ROT_DIM // 2, 1) * s1 + pltpu.roll(t, ROT_DIM // 2, 1) * s2


def _norm_mod(x, nw, sc, sh):
    ms = jnp.mean(x * x, axis=-1, keepdims=True)
    return (x * lax.rsqrt(ms + NORM_EPS) * nw) * (1.0 + sc) + sh


def _dense_kernel(x_ref, w_ref, b_ref, o_ref):
    o_ref[...] = jnp.dot(x_ref[...], w_ref[...], preferred_element_type=f32,
                         precision=lax.Precision.HIGHEST) + b_ref[...]


def _dense(x, w, b, tn=1024):
    m, k = x.shape
    n = w.shape[1]
    return pl.pallas_call(
        _dense_kernel,
        grid=(n // tn,),
        in_specs=[pl.BlockSpec((m, k), lambda j: (0, 0)),
                  pl.BlockSpec((k, tn), lambda j: (0, j)),
                  pl.BlockSpec((1, tn), lambda j: (0, j))],
        out_specs=pl.BlockSpec((m, tn), lambda j: (0, j)),
        out_shape=jax.ShapeDtypeStruct((m, n), f32),
        compiler_params=_params(("parallel",)),
        name="adaln_dense",
    )(x, w, b)


def _nmm_kernel(x_ref, nw_ref, sc_ref, sh_ref, w_ref, c_ref, s1_ref, s2_ref, o_ref, *, rot_lo, rot_hi):
    h = _norm_mod(x_ref[...], nw_ref[...], sc_ref[0], sh_ref[0])
    y = jnp.dot(h.astype(bf16), w_ref[...], preferred_element_type=f32)
    c, s1, s2 = c_ref[...], s1_ref[...], s2_ref[...]
    for j in range(y.shape[1] // LANES):
        yc = y[:, j * LANES:(j + 1) * LANES]
        if rot_lo <= j < rot_hi:
            yc = _rot(yc, c, s1, s2)
        o_ref[:, j * LANES:(j + 1) * LANES] = yc


def _norm_mod_matmul(x2, nw, sc, sh, w, tabs, seq, rot_lo, rot_hi, tm=256):
    m, d = x2.shape
    n = w.shape[1]
    tpb = seq // tm
    row = lambda i: (i, 0)
    per_b = lambda i: (i // tpb, 0, 0)
    return pl.pallas_call(
        functools.partial(_nmm_kernel, rot_lo=rot_lo, rot_hi=rot_hi),
        grid=(m // tm,),
        in_specs=[pl.BlockSpec((tm, d), row),
                  pl.BlockSpec((1, d), lambda i: (0, 0)),
                  pl.BlockSpec((1, 1, d), per_b),
                  pl.BlockSpec((1, 1, d), per_b),
                  pl.BlockSpec((d, n), lambda i: (0, 0)),
                  pl.BlockSpec((tm, LANES), row),
                  pl.BlockSpec((tm, LANES), row),
                  pl.BlockSpec((tm, LANES), row)],
        out_specs=pl.BlockSpec((tm, n), row),
        out_shape=jax.ShapeDtypeStruct((m, n), f32),
        compiler_params=_params(("parallel",)),
        name="norm_mod_proj",
    )(x2, nw, sc, sh, w, *tabs)


def _swa_kernel(sink_ref, q_ref, kp_ref, kc_ref, vp_ref, vc_ref, o_ref):
    i = pl.program_id(1)
    w = WINDOW
    g_sz = A_HEADS // A_KV_HEADS
    q = q_ref[...]
    row = lax.broadcasted_iota(i32, (g_sz * w, 2 * w), 0)
    col = lax.broadcasted_iota(i32, (g_sz * w, 2 * w), 1)
    dist = (row & (w - 1)) + w - col
    lo = jnp.where(i > 0, 0, w)
    valid = (dist >= 0) & (dist < w) & (col >= lo)
    hrow = lax.broadcasted_iota(i32, (g_sz * w, 1), 0) // w
    outs = []
    for g in range(A_KV_HEADS):
        cs = slice(g * HEAD_DIM, (g + 1) * HEAD_DIM)
        kk = jnp.concatenate([kp_ref[:, cs], kc_ref[:, cs]], axis=0).astype(bf16)
        vv = jnp.concatenate([vp_ref[:, cs], vc_ref[:, cs]], axis=0).astype(bf16)
        qg = jnp.concatenate([q[:, (g * g_sz + hl) * HEAD_DIM:(g * g_sz + hl + 1) * HEAD_DIM]
                              for hl in range(g_sz)], axis=0).astype(bf16)
        s = lax.dot_general(qg, kk, _NT, preferred_element_type=f32) * (HEAD_DIM ** -0.5)
        s = jnp.where(valid, s, -jnp.inf)
        sink = jnp.zeros((g_sz * w, 1), f32)
        for hl in range(g_sz):
            sink = jnp.where(hrow == hl, sink_ref[g * g_sz + hl], sink)
        mx = jnp.maximum(jnp.max(s, axis=-1, keepdims=True), sink)
        p = jnp.exp(s - mx)
        p = p / (jnp.sum(p, axis=-1, keepdims=True) + jnp.exp(sink - mx))
        o = jnp.dot(p.astype(bf16), vv, preferred_element_type=f32)
        for hl in range(g_sz):
            outs.append(o[hl * w:(hl + 1) * w])
    o_ref[...] = jnp.concatenate(outs, axis=1).astype(o_ref.dtype)


def _swa_attention(proj, sinks, batch, seq):
    nb = seq // WINDOW
    cur = lambda cb: (lambda b, i: (b * nb + i, cb))
    prev = lambda cb: (lambda b, i: (b * nb + jnp.maximum(i - 1, 0), cb))
    kcb, vcb = AB_K0 // LANES, AB_V0 // LANES
    return pl.pallas_call(
        _swa_kernel,
        grid=(batch, nb),
        in_specs=[pl.BlockSpec(memory_space=pltpu.SMEM),
                  pl.BlockSpec((WINDOW, A_HEADS * HEAD_DIM), cur(0)),
                  pl.BlockSpec((WINDOW, LANES), prev(kcb)),
                  pl.BlockSpec((WINDOW, LANES), cur(kcb)),
                  pl.BlockSpec((WINDOW, LANES), prev(vcb)),
                  pl.BlockSpec((WINDOW, LANES), cur(vcb))],
        out_specs=pl.BlockSpec((WINDOW, A_HEADS * HEAD_DIM), cur(0)),
        out_shape=jax.ShapeDtypeStruct((batch * seq, A_HEADS * HEAD_DIM), bf16),
        compiler_params=_params(("parallel", "parallel")),
        name="swa_attention",
    )(sinks, proj, proj, proj, proj, proj)


def _hgrn_kernel(q_ref, f_ref, i_ref, g_ref, lb_ref, nw_ref, o_ref, st_ref, *, n_sub):
    @pl.when(pl.program_id(2) == 0)
    def _():
        st_ref[...] = jnp.zeros_like(st_ref)

    c = B_CHUNK
    row = lax.broadcasted_iota(i32, (c, B_KEY_DIM), 0)
    r2 = lax.broadcasted_iota(i32, (c, c), 0)
    c2 = lax.broadcasted_iota(i32, (c, c), 1)
    lb = lb_ref[0]
    nw = nw_ref[...]

    def chunk(j, carry):
        r0 = pl.multiple_of(j * c, c)
        rows = pl.ds(r0, c)
        f = lb + (1.0 - lb) * jax.nn.sigmoid(f_ref[rows, :])
        g = jnp.log(f)
        kin = 1.0 - f
        q = q_ref[rows, :]
        qf = q * jax.nn.sigmoid(q) * (B_KEY_DIM ** -0.5)
        iv = i_ref[rows, :].astype(bf16)

        b = g
        d = 1
        while d < c:
            b = b + jnp.where(row >= d, pltpu.roll(b, d, 0), 0.0)
            d *= 2

        scores = jnp.where(r2 == c2, jnp.sum(qf * kin, axis=-1, keepdims=True), 0.0)
        bm = jnp.where(row >= 1, pltpu.roll(b, 1, 0), 0.0)
        m = 1
        while m < c:
            if m > 1:
                bm = jnp.where((row & (m - 1)) < m // 2, bm, pltpu.roll(bm, m // 2, 0))
            right = ((row // m) & 1) == 1
            e = jnp.where(right, b - bm, pltpu.roll(bm, c - m, 0) - b)
            xm = (jnp.where(right, qf, kin) * jnp.exp(e)).astype(bf16)
            y = lax.dot_general(xm, xm, _NT, preferred_element_type=f32)
            pair = (((r2 // m) & 1) == 1) & ((c2 // m) == (r2 // m) - 1)
            scores = scores + jnp.where(pair, y, 0.0)
            m *= 2

        bl = b[c - 1:c, :]
        st = st_ref[...]
        inter = lax.dot_general((qf * jnp.exp(b)).astype(bf16), st.astype(bf16), _NT,
                                preferred_element_type=f32)
        intra = jnp.dot(scores.astype(bf16), iv, preferred_element_type=f32)
        kl = (kin * jnp.exp(bl - b)).astype(bf16)
        st_ref[...] = st * jnp.exp(bl) + lax.dot_general(iv, kl, _TN, preferred_element_type=f32)
        o = inter + intra
        o = o * lax.rsqrt(jnp.mean(o * o, axis=-1, keepdims=True) + NORM_EPS) * nw
        gt = g_ref[rows, :]
        o_ref[rows, :] = (o * (gt * jax.nn.sigmoid(gt))).astype(o_ref.dtype)
        return carry

    lax.fori_loop(0, n_sub, chunk, 0)


def _hgrn2(proj, lb, onorm_w, batch, seq, rows=512):
    n_sub = rows // B_CHUNK
    nr = seq // rows
    blk = lambda c0: pl.BlockSpec((rows, B_KEY_DIM), lambda b, h, r: (b * nr + r, c0 // LANES + h))
    return pl.pallas_call(
        functools.partial(_hgrn_kernel, n_sub=n_sub),
        grid=(batch, B_HEADS, nr),
        in_specs=[blk(AB_BQ0), blk(AB_BF0), blk(AB_BI0), blk(AB_BG0),
                  pl.BlockSpec((1, 1, B_KEY_DIM), lambda b, h, r: (h, 0, 0)),
                  pl.BlockSpec((1, B_VAL_DIM), lambda b, h, r: (0, 0))],
        out_specs=pl.BlockSpec((rows, B_VAL_DIM), lambda b, h, r: (b * nr + r, h)),
        out_shape=jax.ShapeDtypeStruct((batch * seq, B_HEADS * B_VAL_DIM), bf16),
        scratch_shapes=[pltpu.VMEM((B_VAL_DIM, B_KEY_DIM), f32)],
        compiler_params=_params(("parallel", "parallel", "arbitrary")),
        name="hgrn2",
    )(proj, proj, proj, proj, lb, onorm_w)


def _proj_res_kernel(*refs, n_lhs):
    lhs, ws = refs[:n_lhs], refs[n_lhs:2 * n_lhs]
    x_ref, g_ref, o_ref = refs[2 * n_lhs:]
    acc = jnp.dot(lhs[0][...], ws[0][...], preferred_element_type=f32)
    for a, w in zip(lhs[1:], ws[1:]):
        acc = acc + jnp.dot(a[...], w[...], preferred_element_type=f32)
    o_ref[...] = x_ref[...] + g_ref[0] * acc


def _proj_residual(lhs_list, w, x2, gate, seq, tm=512):
    m, d = x2.shape
    kp = lhs_list[0].shape[1]
    n_lhs = len(lhs_list)
    tpb = seq // tm
    row = lambda i: (i, 0)
    in_specs = ([pl.BlockSpec((tm, kp), row) for _ in lhs_list]
                + [pl.BlockSpec((kp, d), (lambda p: (lambda i: (p, 0)))(p)) for p in range(n_lhs)]
                + [pl.BlockSpec((tm, d), row), pl.BlockSpec((1, 1, d), lambda i: (i // tpb, 0, 0))])
    return pl.pallas_call(
        functools.partial(_proj_res_kernel, n_lhs=n_lhs),
        grid=(m // tm,),
        in_specs=in_specs,
        out_specs=pl.BlockSpec((tm, d), row),
        out_shape=jax.ShapeDtypeStruct((m, d), f32),
        compiler_params=_params(("parallel",)),
        name="out_proj_residual",
    )(*lhs_list, *([w] * n_lhs), x2, gate)


def _router_kernel(x_ref, nw_ref, sc_ref, sh_ref, rw_ref, rb_ref,
                   h_ref, eid_ref, gate_ref, rank_ref, cnt_ref, carry_ref):
    @pl.when(pl.program_id(0) == 0)
    def _():
        carry_ref[...] = jnp.zeros_like(carry_ref)

    h = _norm_mod(x_ref[...], nw_ref[...], sc_ref[0], sh_ref[0])
    h_ref[...] = h
    tm = h.shape[0]
    logits = jnp.dot(h, rw_ref[...], preferred_element_type=f32,
                     precision=lax.Precision.HIGHEST) + rb_ref[...]
    lane = lax.broadcasted_iota(i32, (tm, LANES), 1)
    work = jnp.where(lane < N_EXPERTS, logits, -jnp.inf)
    hots, vals, ids = [], [], []
    for _ in range(TOP_K):
        mx = jnp.max(work, axis=-1, keepdims=True)
        idx = jnp.min(jnp.where(work == mx, lane, LANES), axis=-1, keepdims=True)
        hot = lane == idx
        work = jnp.where(hot, -jnp.inf, work)
        hots.append(hot)
        vals.append(mx)
        ids.append(idx)
    es = [jnp.exp(v - vals[0]) for v in vals]
    denom = es[0] + es[1] + es[2] + es[3]
    member = jnp.zeros((tm, LANES), f32)
    for hot in hots:
        member = member + jnp.where(hot, 1.0, 0.0)
    r2 = lax.broadcasted_iota(i32, (tm, tm), 0)
    c2 = lax.broadcasted_iota(i32, (tm, tm), 1)
    before = jnp.where(c2 < r2, 1.0, 0.0).astype(bf16)
    prior = jnp.dot(before, member.astype(bf16), preferred_element_type=f32) + carry_ref[...]
    eid = jnp.zeros((tm, LANES), i32)
    gate = jnp.zeros((tm, LANES), f32)
    rank = jnp.zeros((tm, LANES), i32)
    for k in range(TOP_K):
        rk = jnp.sum(jnp.where(hots[k], prior, 0.0), axis=-1, keepdims=True)
        eid = jnp.where(lane == k, ids[k], eid)
        gate = jnp.where(lane == k, es[k] / denom, gate)
        rank = jnp.where(lane == k, rk.astype(i32), rank)
    eid_ref[...] = eid
    gate_ref[...] = gate
    rank_ref[...] = rank
    carry_ref[...] = carry_ref[...] + jnp.sum(member, axis=0, keepdims=True)
    cnt_ref[...] = carry_ref[...]


def _router(x2, nw, sc, sh, rw, rb, seq, tm=512):
    m, d = x2.shape
    tpb = seq // tm
    row = lambda i: (i, 0)
    per_b = lambda i: (i // tpb, 0, 0)
    fixed = lambda i: (0, 0)
    return pl.pallas_call(
        _router_kernel,
        grid=(m // tm,),
        in_specs=[pl.BlockSpec((tm, d), row),
                  pl.BlockSpec((1, d), fixed),
                  pl.BlockSpec((1, 1, d), per_b),
                  pl.BlockSpec((1, 1, d), per_b),
                  pl.BlockSpec((d, LANES), fixed),
                  pl.BlockSpec((1, LANES), fixed)],
        out_specs=[pl.BlockSpec((tm, d), row),
                   pl.BlockSpec((tm, LANES), row),
                   pl.BlockSpec((tm, LANES), row),
                   pl.BlockSpec((tm, LANES), row),
                   pl.BlockSpec((1, LANES), fixed)],
        out_shape=[jax.ShapeDtypeStruct((m, d), f32),
                   jax.ShapeDtypeStruct((m, LANES), i32),
                   jax.ShapeDtypeStruct((m, LANES), f32),
                   jax.ShapeDtypeStruct((m, LANES), i32),
                   jax.ShapeDtypeStruct((1, LANES), f32)],
        scratch_shapes=[pltpu.VMEM((1, LANES), f32)],
        compiler_params=_params(("arbitrary",)),
        name="ffn_norm_router",
    )(x2, nw, sc, sh, rw, rb)


def _dispatch_kernel(pos_ref, h_ref, zin_ref, out_ref, sem):
    del zin_ref
    tm = h_ref.shape[0]

    def copy(r, k):
        return pltpu.make_async_copy(h_ref.at[pl.ds(r, 1), :],
                                     out_ref.at[pl.ds(pos_ref[TOP_K * r + k], 1), :], sem)

    def issue(r, carry):
        for k in range(TOP_K):
            copy(r, k).start()
        return carry

    def drain(r, carry):
        for k in range(TOP_K):
            copy(r, k).wait()
        return carry

    lax.fori_loop(0, tm, issue, 0)
    lax.fori_loop(0, tm, drain, 0)


def _dispatch(h2, pos_flat, n_rows, tm=512):
    m, d = h2.shape
    zeros = jnp.zeros((n_rows, d), f32)
    return pl.pallas_call(
        _dispatch_kernel,
        grid=(m // tm,),
        in_specs=[pl.BlockSpec((tm * TOP_K,), lambda i: (i,), memory_space=pltpu.SMEM),
                  pl.BlockSpec((tm, d), lambda i: (i, 0)),
                  pl.BlockSpec(memory_space=pl.ANY)],
        out_specs=pl.BlockSpec(memory_space=pl.ANY),
        out_shape=jax.ShapeDtypeStruct((n_rows, d), f32),
        scratch_shapes=[pltpu.SemaphoreType.DMA(())],
        input_output_aliases={2: 0},
        compiler_params=_params(("arbitrary",)),
        name="moe_dispatch",
    )(pos_flat, h2, zeros)


def _expert_kernel(te_ref, nu_ref, x_ref, w1_ref, b1_ref, w2_ref, b2_ref, o_ref, w1b_ref, w2b_ref):
    i = pl.program_id(0)
    e = te_ref[i]
    e_prev = te_ref[jnp.maximum(i - 1, 0)]

    @pl.when(jnp.logical_or(i == 0, e != e_prev))
    def _():
        w1b_ref[...] = w1_ref[0].astype(bf16)
        w2b_ref[...] = w2_ref[0].astype(bf16)

    @pl.when(i < nu_ref[0])
    def _():
        x = x_ref[...].astype(bf16)
        gu = jnp.dot(x, w1b_ref[...], preferred_element_type=f32) + b1_ref[0]
        g = jnp.minimum(gu[:, :D_FF], SWIGLU_LIMIT)
        u = jnp.clip(gu[:, D_FF:], -SWIGLU_LIMIT, SWIGLU_LIMIT)
        act = g * jax.nn.sigmoid(SWIGLU_ALPHA * g) * (u + 1.0)
        o_ref[...] = jnp.dot(act.astype(bf16), w2b_ref[...], preferred_element_type=f32) + b2_ref[0]

    @pl.when(i >= nu_ref[0])
    def _():
        o_ref[...] = jnp.zeros_like(o_ref)


def _expert_ffn(xs, tile_expert, n_used, w1, b1, w2, b2):
    n_rows, d = xs.shape
    tm = EXPERT_TILE
    by_e = lambda i, te, nu: (te[i], 0, 0)
    return pl.pallas_call(
        _expert_kernel,
        grid_spec=pltpu.PrefetchScalarGridSpec(
            num_scalar_prefetch=2,
            grid=(n_rows // tm,),
            in_specs=[pl.BlockSpec((tm, d), lambda i, te, nu: (i, 0)),
                      pl.BlockSpec((1, d, 2 * D_FF), by_e),
                      pl.BlockSpec((1, 1, 2 * D_FF), by_e),
                      pl.BlockSpec((1, D_FF, d), by_e),
                      pl.BlockSpec((1, 1, d), by_e)],
            out_specs=pl.BlockSpec((tm, d), lambda i, te, nu: (i, 0)),
            scratch_shapes=[pltpu.VMEM((d, 2 * D_FF), bf16), pltpu.VMEM((D_FF, d), bf16)]),
        out_shape=jax.ShapeDtypeStruct((n_rows, d), f32),
        compiler_params=_params(("arbitrary",)),
        name="expert_ffn",
    )(tile_expert, n_used, xs, w1, b1, w2, b2)


def _combine_kernel(pos_ref, ys_ref, gate_ref, x_ref, g2_ref, fw_ref, o_ref, buf_ref, sem, *, final_norm):
    tm = x_ref.shape[0]

    def copy(r, k):
        return pltpu.make_async_copy(ys_ref.at[pl.ds(pos_ref[TOP_K * r + k], 1), :],
                                     buf_ref.at[k, pl.ds(r, 1), :], sem)

    def issue(r, carry):
        for k in range(TOP_K):
            copy(r, k).start()
        return carry

    def drain(r, carry):
        for k in range(TOP_K):
            copy(r, k).wait()
        return carry

    lax.fori_loop(0, tm, issue, 0)
    lax.fori_loop(0, tm, drain, 0)
    gates = gate_ref[...]
    acc = gates[:, 0:1] * buf_ref[0]
    for k in range(1, TOP_K):
        acc = acc + gates[:, k:k + 1] * buf_ref[k]
    y = x_ref[...] + g2_ref[0] * acc
    if final_norm:
        y = y * lax.rsqrt(jnp.mean(y * y, axis=-1, keepdims=True) + NORM_EPS) * fw_ref[...]
    o_ref[...] = y


def _combine(ys, pos_flat, gates, x2, g2, final_w, seq, final_norm, tm=128):
    m, d = x2.shape
    tpb = seq // tm
    row = lambda i: (i, 0)
    return pl.pallas_call(
        functools.partial(_combine_kernel, final_norm=final_norm),
        grid=(m // tm,),
        in_specs=[pl.BlockSpec((tm * TOP_K,), lambda i: (i,), memory_space=pltpu.SMEM),
                  pl.BlockSpec(memory_space=pl.ANY),
                  pl.BlockSpec((tm, LANES), row),
                  pl.BlockSpec((tm, d), row),
                  pl.BlockSpec((1, 1, d), lambda i: (i // tpb, 0, 0)),
                  pl.BlockSpec((1, d), lambda i: (0, 0))],
        out_specs=pl.BlockSpec((tm, d), row),
        out_shape=jax.ShapeDtypeStruct((m, d), f32),
        scratch_shapes=[pltpu.VMEM((TOP_K, tm, d), f32), pltpu.SemaphoreType.DMA(())],
        compiler_params=_params(("arbitrary",)),
        name="moe_combine",
    )(pos_flat, ys, gates, x2, g2, final_w)


def _routed_ffn(x2, nw, sc, sh, g2, rw, rb, w1, b1, w2, b2, final_w, seq, final_norm):
    m, d = x2.shape
    rw_pad = jnp.pad(rw, ((0, 0), (0, LANES - N_EXPERTS)))
    rb_pad = jnp.pad(rb, (0, LANES - N_EXPERTS)).reshape(1, LANES)
    h2, eid, gates, rank, cnt = _router(x2, nw, sc, sh, rw_pad, rb_pad, seq)
    tm = EXPERT_TILE
    n_tiles = (m * TOP_K) // tm + N_EXPERTS
    counts = cnt[0, :N_EXPERTS].astype(i32)
    padded = ((counts + tm - 1) // tm) * tm
    gend = jnp.cumsum(padded)
    gstart = gend - padded
    n_used = (gend[-1] // tm).astype(i32).reshape(1)
    tile_expert = jnp.searchsorted(gend, jnp.arange(n_tiles, dtype=i32) * tm, side="right")
    tile_expert = jnp.minimum(tile_expert, N_EXPERTS - 1).astype(i32)
    pos = gstart[eid[:, :TOP_K]] + rank[:, :TOP_K]
    pos_flat = pos.reshape(-1).astype(i32)
    xs = _dispatch(h2, pos_flat, n_tiles * tm)
    ys = _expert_ffn(xs, tile_expert, n_used, w1, b1.reshape(N_EXPERTS, 1, 2 * D_FF),
                     w2, b2.reshape(N_EXPERTS, 1, d))
    return _combine(ys, pos_flat, gates, x2, g2, final_w, seq, final_norm)


def _dsa_prep_kernel(y_ref, qn_ref, wuq_ref, wiq_ref, lnw_ref, lnb_ref, c_ref, s1_ref, s2_ref,
                     q_ref, k_ref, v_ref, qi_ref, ki_ref, wi_ref):
    y = y_ref[...]
    c, s1, s2 = c_ref[...], s1_ref[...], s2_ref[...]
    cq = y[:, :C_Q_RANK]
    cq = (cq * lax.rsqrt(jnp.mean(cq * cq, axis=-1, keepdims=True) + NORM_EPS) * qn_ref[...]).astype(bf16)
    q = jnp.dot(cq, wuq_ref[...], preferred_element_type=f32)
    for j in range(q.shape[1] // LANES):
        sl = slice(j * LANES, (j + 1) * LANES)
        q_ref[:, sl] = (_rot(q[:, sl], c, s1, s2) * (HEAD_DIM ** -0.5)).astype(bf16)
    qi = jnp.dot(cq, wiq_ref[...], preferred_element_type=f32)
    for j in range(qi.shape[1] // LANES):
        sl = slice(j * LANES, (j + 1) * LANES)
        qi_ref[:, sl] = (_rot(qi[:, sl], c, s1, s2) * (IDX_DIM ** -0.5)).astype(bf16)
    k_ref[...] = y[:, C_K0:C_V0].astype(bf16)
    v_ref[...] = y[:, C_V0:C_IK0].astype(bf16)
    t6 = y[:, C_IK0:C_IK0 + LANES]
    ik = t6[:, :IDX_DIM]
    mu = jnp.mean(ik, axis=-1, keepdims=True)
    var = jnp.mean(jnp.square(ik - mu), axis=-1, keepdims=True)
    ln = (ik - mu) * lax.rsqrt(var + NORM_EPS) * lnw_ref[...] + lnb_ref[...]
    ln = jnp.concatenate([ln, jnp.zeros_like(ln)], axis=1)
    ki_ref[...] = _rot(ln, c, s1, s2)[:, :IDX_DIM].astype(bf16)
    wi_ref[...] = t6[:, IDX_DIM:IDX_DIM + IDX_HEADS] * (IDX_HEADS ** -0.5)


def _dsa_prep(y, qn, wuq, wiq, lnw, lnb, tabs, tm=256):
    m = y.shape[0]
    row = lambda i: (i, 0)
    fixed = lambda i: (0, 0)
    kvw = C_KV_HEADS * HEAD_DIM
    return pl.pallas_call(
        _dsa_prep_kernel,
        grid=(m // tm,),
        in_specs=[pl.BlockSpec((tm, C_IN_PAD), row),
                  pl.BlockSpec((1, C_Q_RANK), fixed),
                  pl.BlockSpec(wuq.shape, fixed),
                  pl.BlockSpec(wiq.shape, fixed),
                  pl.BlockSpec((1, IDX_DIM), fixed),
                  pl.BlockSpec((1, IDX_DIM), fixed),
                  pl.BlockSpec((tm, LANES), row),
                  pl.BlockSpec((tm, LANES), row),
                  pl.BlockSpec((tm, LANES), row)],
        out_specs=[pl.BlockSpec((tm, C_HEADS * HEAD_DIM), row),
                   pl.BlockSpec((tm, kvw), row),
                   pl.BlockSpec((tm, kvw), row),
                   pl.BlockSpec((tm, IDX_HEADS * IDX_DIM), row),
                   pl.BlockSpec((tm, IDX_DIM), row),
                   pl.BlockSpec((tm, IDX_HEADS), row)],
        out_shape=[jax.ShapeDtypeStruct((m, C_HEADS * HEAD_DIM), bf16),
                   jax.ShapeDtypeStruct((m, kvw), bf16),
                   jax.ShapeDtypeStruct((m, kvw), bf16),
                   jax.ShapeDtypeStruct((m, IDX_HEADS * IDX_DIM), bf16),
                   jax.ShapeDtypeStruct((m, IDX_DIM), bf16),
                   jax.ShapeDtypeStruct((m, IDX_HEADS), f32)],
        compiler_params=_params(("parallel",)),
        name="dsa_prep",
    )(y, qn, wuq, wiq, lnw, lnb, *tabs)


def _dsa_select_kernel(qi_ref, wi_ref, ki_ref, mask_ref, key_ref, *, seq, n_sel):
    i = pl.program_id(1)
    qb, kc = C_QBLOCK, KEY_CHUNK
    nkc = (i * qb + qb + kc - 1) // kc
    qs = jnp.concatenate([qi_ref[:, h * IDX_DIM:(h + 1) * IDX_DIM] for h in range(IDX_HEADS)], axis=0)
    wi = wi_ref[...]
    qpos = i * qb + lax.broadcasted_iota(i32, (qb, kc), 0)
    kiota = lax.broadcasted_iota(i32, (qb, kc), 1)

    def score_chunk(c, carry):
        k0 = pl.multiple_of(c * kc, kc)
        lg = lax.dot_general(qs, ki_ref[pl.ds(k0, kc), :], _NT, preferred_element_type=f32)
        sc = wi[:, 0:1] * jnp.maximum(lg[0:qb], 0.0)
        for h in range(1, IDX_HEADS):
            sc = sc + wi[:, h:h + 1] * jnp.maximum(lg[h * qb:(h + 1) * qb], 0.0)
        sc = jnp.where(sc == 0.0, 0.0, sc)
        sc = jnp.where(k0 + kiota <= qpos, sc, -jnp.inf)
        bits = pltpu.bitcast(sc, i32)
        key_ref[:, pl.ds(k0, kc)] = bits ^ ((bits >> 31) & 0x7FFFFFFF)
        return carry

    lax.fori_loop(0, nkc, score_chunk, 0)

    def count(pred):
        def body(c, acc):
            k0 = pl.multiple_of(c * kc, kc)
            hit = jnp.where(pred(key_ref[:, pl.ds(k0, kc)]), 1, 0)
            for j in range(kc // LANES):
                acc = acc + hit[:, j * LANES:(j + 1) * LANES]
            return acc
        acc = lax.fori_loop(0, nkc, body, jnp.zeros((qb, LANES), i32))
        return jnp.sum(acc, axis=-1, keepdims=True)

    thr = jnp.where(count(lambda k: k >= 0) >= n_sel, 0, INT_MIN).astype(i32)

    def bit_step(bi, thr):
        cand = thr | jnp.left_shift(jnp.int32(1), 30 - bi)
        return jnp.where(count(lambda k: k >= cand) >= n_sel, cand, thr)

    thr = lax.fori_loop(0, 31, bit_step, thr)
    allowed = (n_sel - count(lambda k: k > thr)).astype(f32)
    r2 = lax.broadcasted_iota(i32, (kc, kc), 0)
    c2 = lax.broadcasted_iota(i32, (kc, kc), 1)
    upto = jnp.where(r2 <= c2, 1.0, 0.0).astype(bf16)

    def out_chunk(c, ties_before):
        k0 = pl.multiple_of(c * kc, kc)
        key = key_ref[:, pl.ds(k0, kc)]
        eq = jnp.where(key == thr, 1.0, 0.0)
        tie_rank = jnp.dot(eq.astype(bf16), upto, preferred_element_type=f32) + ties_before
        tie_ok = (key == thr) & (tie_rank <= allowed)
        sel = (k0 + kiota <= qpos) & ((key > thr) | tie_ok)
        mask_ref[0, :, pl.ds(k0, kc)] = jnp.where(sel, 1, 0).astype(jnp.int8)
        return ties_before + jnp.sum(eq, axis=-1, keepdims=True)

    lax.fori_loop(0, nkc, out_chunk, jnp.zeros((qb, 1), f32))

    def zero_chunk(c, carry):
        k0 = pl.multiple_of(c * kc, kc)
        mask_ref[0, :, pl.ds(k0, kc)] = jnp.zeros((qb, kc), jnp.int8)
        return carry

    lax.fori_loop(nkc, seq // kc, zero_chunk, 0)


def _dsa_select(qi, wi, ki, batch, seq):
    nq = seq // C_QBLOCK
    n_sel = min(IDX_TOPK, seq // 4)
    return pl.pallas_call(
        functools.partial(_dsa_select_kernel, seq=seq, n_sel=n_sel),
        grid=(batch, nq),
        in_specs=[pl.BlockSpec((C_QBLOCK, IDX_HEADS * IDX_DIM), lambda b, i: (b * nq + i, 0)),
                  pl.BlockSpec((C_QBLOCK, IDX_HEADS), lambda b, i: (b * nq + i, 0)),
                  pl.BlockSpec((seq, IDX_DIM), lambda b, i: (b, 0))],
        out_specs=pl.BlockSpec((1, C_QBLOCK, seq), lambda b, i: (b, i, 0)),
        out_shape=jax.ShapeDtypeStruct((batch, seq, seq), jnp.int8),
        scratch_shapes=[pltpu.VMEM((C_QBLOCK, seq), i32)],
        compiler_params=_params(("parallel", "parallel")),
        name="dsa_select",
    )(qi, wi, ki)


def _dsa_attn_kernel(q_ref, k_ref, v_ref, mask_ref, o_ref, m_ref, l_ref, acc_ref):
    i = pl.program_id(1)
    qb, kc = C_QBLOCK, KEY_CHUNK
    g_sz = C_HEADS // C_KV_HEADS
    nkc = (i * qb + qb + kc - 1) // kc
    m_ref[...] = jnp.full_like(m_ref, NEG_BIG)
    l_ref[...] = jnp.zeros_like(l_ref)
    acc_ref[...] = jnp.zeros_like(acc_ref)
    qg = [jnp.concatenate([q_ref[:, (g * g_sz + hl) * HEAD_DIM:(g * g_sz + hl + 1) * HEAD_DIM]
                           for hl in range(g_sz)], axis=0) for g in range(C_KV_HEADS)]

    def body(c, carry):
        k0 = pl.multiple_of(c * kc, kc)
        keep = mask_ref[0, :, pl.ds(k0, kc)].astype(i32) != 0
        keep = jnp.concatenate([keep] * g_sz, axis=0)
        for g in range(C_KV_HEADS):
            cs = slice(g * HEAD_DIM, (g + 1) * HEAD_DIM)
            s = lax.dot_general(qg[g], k_ref[pl.ds(k0, kc), cs], _NT, preferred_element_type=f32)
            s = jnp.where(keep, s, -jnp.inf)
            m_old = m_ref[g]
            m_new = jnp.maximum(m_old, jnp.max(s, axis=-1, keepdims=True))
            p = jnp.exp(s - m_new)
            alpha = jnp.exp(m_old - m_new)
            l_ref[g] = alpha * l_ref[g] + jnp.sum(p, axis=-1, keepdims=True)
            acc_ref[g] = alpha * acc_ref[g] + jnp.dot(p.astype(bf16), v_ref[pl.ds(k0, kc), cs],
                                                      preferred_element_type=f32)
            m_ref[g] = m_new
        return carry

    lax.fori_loop(0, nkc, body, 0)
    outs = []
    for g in range(C_KV_HEADS):
        o = acc_ref[g] / l_ref[g]
        for hl in range(g_sz):
            outs.append(o[hl * qb:(hl + 1) * qb])
    o_ref[...] = jnp.concatenate(outs, axis=1).astype(o_ref.dtype)


def _dsa_attention(q, k, v, mask, batch, seq):
    nq = seq // C_QBLOCK
    g_rows = (C_HEADS // C_KV_HEADS) * C_QBLOCK
    kvw = C_KV_HEADS * HEAD_DIM
    return pl.pallas_call(
        _dsa_attn_kernel,
        grid=(batch, nq),
        in_specs=[pl.BlockSpec((C_QBLOCK, C_HEADS * HEAD_DIM), lambda b, i: (b * nq + i, 0)),
                  pl.BlockSpec((seq, kvw), lambda b, i: (b, 0)),
                  pl.BlockSpec((seq, kvw), lambda b, i: (b, 0)),
                  pl.BlockSpec((1, C_QBLOCK, seq), lambda b, i: (b, i, 0))],
        out_specs=pl.BlockSpec((C_QBLOCK, C_HEADS * HEAD_DIM), lambda b, i: (b * nq + i, 0)),
        out_shape=jax.ShapeDtypeStruct((batch * seq, C_HEADS * HEAD_DIM), bf16),
        scratch_shapes=[pltpu.VMEM((C_KV_HEADS, g_rows, 1), f32),
                        pltpu.VMEM((C_KV_HEADS, g_rows, 1), f32),
                        pltpu.VMEM((C_KV_HEADS, g_rows, HEAD_DIM), f32)],
        compiler_params=_params(("parallel", "parallel")),
        name="dsa_attention",
    )(q, k, v, mask)


def _rotary_tables(positions, rot_dim):
    half = rot_dim // 2
    inv_freq = ROPE_THETA ** (-jnp.arange(0, rot_dim, 2, dtype=f32) / rot_dim)
    ang = positions.astype(f32).reshape(-1, 1) * inv_freq
    cos, sin = jnp.cos(ang), jnp.sin(ang)
    n = ang.shape[0]
    rest = HEAD_DIM - 2 * half
    c = jnp.concatenate([cos, cos, jnp.ones((n, rest), f32)], axis=-1)
    s1 = jnp.concatenate([-sin, jnp.zeros((n, half + rest), f32)], axis=-1)
    s2 = jnp.concatenate([jnp.zeros((n, half), f32), sin, jnp.zeros((n, rest), f32)], axis=-1)
    rep = LANES // HEAD_DIM
    return tuple(jnp.tile(t, (1, rep)) for t in (c, s1, s2))


def kernel(x, c, positions, mod_w, mod_b, norm_mix_w, norm_ffn_w, ab_w_in, ab_w_out, a_sinks, b_lb_logits,
           b_onorm_w, c_w_in, c_q_norm_w, c_w_uq, c_w_iq, c_ik_norm_w, c_ik_norm_b, c_w_out, router_w,
           router_b, moe_w1, moe_b1, moe_w2, moe_b2, final_norm_w):
    batch, seq, d = x.shape
    depth = mod_w.shape[0]
    tabs = _rotary_tables(positions, ROT_DIM)
    tabs_idx = _rotary_tables(positions, IDX_ROT_DIM)
    lower_bounds = jnp.cumsum(jax.nn.softmax(b_lb_logits.astype(f32), axis=0), axis=0)
    cond = jnp.pad(jax.nn.silu(c), ((0, 8 - batch), (0, 0)))
    x2 = x.reshape(batch * seq, d)
    final_w = final_norm_w.reshape(1, d)
    for layer in range(depth):
        mod = _dense(cond, mod_w[layer], mod_b[layer].reshape(1, -1))[:batch]
        sh1, sc1, g1, sh2, sc2, g2 = [t.reshape(batch, 1, d) for t in jnp.split(mod, 6, axis=-1)]
        nw = norm_mix_w[layer].reshape(1, d)
        j = layer // 2
        if layer % 2 == 0:
            proj = _norm_mod_matmul(x2, nw, sc1, sh1, ab_w_in[j].astype(bf16), tabs, seq,
                                    rot_lo=AB_Q0 // LANES, rot_hi=AB_V0 // LANES)
            out_a = _swa_attention(proj, a_sinks[j], batch, seq)
            out_b = _hgrn2(proj, lower_bounds[j].reshape(B_HEADS, 1, B_KEY_DIM),
                           b_onorm_w[j].reshape(1, B_VAL_DIM), batch, seq)
            x2 = _proj_residual([out_a, out_b], ab_w_out[j].astype(bf16), x2, g1, seq)
        else:
            w_in = jnp.pad(c_w_in[j], ((0, 0), (0, C_IN_PAD - c_w_in.shape[-1]))).astype(bf16)
            y = _norm_mod_matmul(x2, nw, sc1, sh1, w_in, tabs, seq,
                                 rot_lo=C_K0 // LANES, rot_hi=C_V0 // LANES)
            q, k, v, qi, ki, wi = _dsa_prep(y, c_q_norm_w[j].reshape(1, -1), c_w_uq[j].astype(bf16),
                                            c_w_iq[j].astype(bf16), c_ik_norm_w[j].reshape(1, -1),
                                            c_ik_norm_b[j].reshape(1, -1), tabs_idx)
            mask = _dsa_select(qi, wi, ki, batch, seq)
            o = _dsa_attention(q, k, v, mask, batch, seq)
            x2 = _proj_residual([o], c_w_out[j].astype(bf16), x2, g1, seq)
        x2 = _routed_ffn(x2, norm_ffn_w[layer].reshape(1, d), sc2, sh2, g2, router_w[layer], router_b[layer],
                         moe_w1[layer], moe_b1[layer], moe_w2[layer], moe_b2[layer], final_w, seq,
                         final_norm=(layer == depth - 1))
    return x2.reshape(batch, seq, d)
```

```python
import functools

import jax
import jax.numpy as jnp
from jax import lax
from jax.experimental import pallas as pl
from jax.experimental.pallas import tpu as pltpu

f32 = jnp.float32
bf16 = jnp.bfloat16
i32 = jnp.int32

D_MODEL = 1024
HEAD_DIM = 64
ROT_DIM = HEAD_DIM // 4
ROPE_THETA = 500000.0
NORM_EPS = 1e-5
A_HEADS = 8
A_KV_HEADS = 2
WINDOW = 128
B_HEADS = 4
B_KEY_DIM = 128
B_VAL_DIM = 128
B_CHUNK = 64
C_HEADS = 16
C_KV_HEADS = 4
C_Q_RANK = 256
IDX_HEADS = 8
IDX_DIM = 64
IDX_ROT_DIM = IDX_DIM // 4
IDX_TOPK = 256
C_QBLOCK = 128
N_EXPERTS = 32
TOP_K = 4
D_FF = D_MODEL
SWIGLU_LIMIT = 7.0
SWIGLU_ALPHA = 1.702

LANES = 128
AB_IN = 2560
AB_Q0, AB_K0, AB_V0 = 0, 512, 640
AB_BQ0, AB_BF0, AB_BI0, AB_BG0 = 768, 1280, 1792, 2304
C_IN_PAD = 896
C_K0, C_V0, C_IK0 = 256, 512, 768

VMEM_LIMIT = 56 * 1024 * 1024
EXPERT_TILE = 256
KEY_CHUNK = 512
NEG_BIG = -1e30
INT_MIN = -(2 ** 31)

_NT = (((1,), (1,)), ((), ()))
_TN = (((0,), (0,)), ((), ()))


def _params(sem, vmem=VMEM_LIMIT):
    return pltpu.CompilerParams(dimension_semantics=sem, vmem_limit_bytes=vmem)


def _rot(t, c, s1, s2):
    return t * c + pltpu.roll(t, LANES - ROT_DIM // 2, 1) * s1 + pltpu.roll(t, ROT_DIM // 2, 1) * s2


def _norm_mod(x, nw, sc, sh):
    ms = jnp.mean(x * x, axis=-1, keepdims=True)
    return (x * lax.rsqrt(ms + NORM_EPS) * nw) * (1.0 + sc) + sh


def _dense_kernel(x_ref, w_ref, b_ref, o_ref):
    o_ref[...] = jnp.dot(x_ref[...], w_ref[...], preferred_element_type=f32,
                         precision=lax.Precision.HIGHEST) + b_ref[...]


def _dense(x, w, b, tn=1024):
    m, k = x.shape
    n = w.shape[1]
    return pl.pallas_call(
        _dense_kernel,
        grid=(n // tn,),
        in_specs=[pl.BlockSpec((m, k), lambda j: (0, 0)),
                  pl.BlockSpec((k, tn), lambda j: (0, j)),
                  pl.BlockSpec((1, tn), lambda j: (0, j))],
        out_specs=pl.BlockSpec((m, tn), lambda j: (0, j)),
        out_shape=jax.ShapeDtypeStruct((m, n), f32),
        compiler_params=_params(("parallel",)),
        name="adaln_dense",
    )(x, w, b)


def _nmm_kernel(x_ref, nw_ref, sc_ref, sh_ref, w_ref, c_ref, s1_ref, s2_ref, o_ref, *, rot_lo, rot_hi):
    h = _norm_mod(x_ref[...], nw_ref[...], sc_ref[0], sh_ref[0])
    y = jnp.dot(h.astype(bf16), w_ref[...], preferred_element_type=f32)
    c, s1, s2 = c_ref[...], s1_ref[...], s2_ref[...]
    for j in range(y.shape[1] // LANES):
        yc = y[:, j * LANES:(j + 1) * LANES]
        if rot_lo <= j < rot_hi:
            yc = _rot(yc, c, s1, s2)
        o_ref[:, j * LANES:(j + 1) * LANES] = yc


def _norm_mod_matmul(x2, nw, sc, sh, w, tabs, seq, rot_lo, rot_hi, tm=256):
    m, d = x2.shape
    n = w.shape[1]
    tpb = seq // tm
    row = lambda i: (i, 0)
    per_b = lambda i: (i // tpb, 0, 0)
    return pl.pallas_call(
        functools.partial(_nmm_kernel, rot_lo=rot_lo, rot_hi=rot_hi),
        grid=(m // tm,),
        in_specs=[pl.BlockSpec((tm, d), row),
                  pl.BlockSpec((1, d), lambda i: (0, 0)),
                  pl.BlockSpec((1, 1, d), per_b),
                  pl.BlockSpec((1, 1, d), per_b),
                  pl.BlockSpec((d, n), lambda i: (0, 0)),
                  pl.BlockSpec((tm, LANES), row),
                  pl.BlockSpec((tm, LANES), row),
                  pl.BlockSpec((tm, LANES), row)],
        out_specs=pl.BlockSpec((tm, n), row),
        out_shape=jax.ShapeDtypeStruct((m, n), f32),
        compiler_params=_params(("parallel",)),
        name="norm_mod_proj",
    )(x2, nw, sc, sh, w, *tabs)


def _swa_kernel(sink_ref, q_ref, kp_ref, kc_ref, vp_ref, vc_ref, o_ref):
    i = pl.program_id(1)
    w = WINDOW
    g_sz = A_HEADS // A_KV_HEADS
    q = q_ref[...]
    row = lax.broadcasted_iota(i32, (g_sz * w, 2 * w), 0)
    col = lax.broadcasted_iota(i32, (g_sz * w, 2 * w), 1)
    dist = (row & (w - 1)) + w - col
    lo = jnp.where(i > 0, 0, w)
    valid = (dist >= 0) & (dist < w) & (col >= lo)
    hrow = lax.broadcasted_iota(i32, (g_sz * w, 1), 0) // w
    outs = []
    for g in range(A_KV_HEADS):
        cs = slice(g * HEAD_DIM, (g + 1) * HEAD_DIM)
        kk = jnp.concatenate([kp_ref[:, cs], kc_ref[:, cs]], axis=0).astype(bf16)
        vv = jnp.concatenate([vp_ref[:, cs], vc_ref[:, cs]], axis=0).astype(bf16)
        qg = jnp.concatenate([q[:, (g * g_sz + hl) * HEAD_DIM:(g * g_sz + hl + 1) * HEAD_DIM]
                              for hl in range(g_sz)], axis=0).astype(bf16)
        s = lax.dot_general(qg, kk, _NT, preferred_element_type=f32) * (HEAD_DIM ** -0.5)
        s = jnp.where(valid, s, -jnp.inf)
        sink = jnp.zeros((g_sz * w, 1), f32)
        for hl in range(g_sz):
            sink = jnp.where(hrow == hl, sink_ref[g * g_sz + hl], sink)
        mx = jnp.maximum(jnp.max(s, axis=-1, keepdims=True), sink)
        p = jnp.exp(s - mx)
        p = p / (jnp.sum(p, axis=-1, keepdims=True) + jnp.exp(sink - mx))
        o = jnp.dot(p.astype(bf16), vv, preferred_element_type=f32)
        for hl in range(g_sz):
            outs.append(o[hl * w:(hl + 1) * w])
    o_ref[...] = jnp.concatenate(outs, axis=1).astype(o_ref.dtype)


def _swa_attention(proj, sinks, batch, seq):
    nb = seq // WINDOW
    cur = lambda cb: (lambda b, i: (b * nb + i, cb))
    prev = lambda cb: (lambda b, i: (b * nb + jnp.maximum(i - 1, 0), cb))
    kcb, vcb = AB_K0 // LANES, AB_V0 // LANES
    return pl.pallas_call(
        _swa_kernel,
        grid=(batch, nb),
        in_specs=[pl.BlockSpec(memory_space=pltpu.SMEM),
                  pl.BlockSpec((WINDOW, A_HEADS * HEAD_DIM), cur(0)),
                  pl.BlockSpec((WINDOW, LANES), prev(kcb)),
                  pl.BlockSpec((WINDOW, LANES), cur(kcb)),
                  pl.BlockSpec((WINDOW, LANES), prev(vcb)),
                  pl.BlockSpec((WINDOW, LANES), cur(vcb))],
        out_specs=pl.BlockSpec((WINDOW, A_HEADS * HEAD_DIM), cur(0)),
        out_shape=jax.ShapeDtypeStruct((batch * seq, A_HEADS * HEAD_DIM), bf16),
        compiler_params=_params(("parallel", "parallel")),
        name="swa_attention",
    )(sinks, proj, proj, proj, proj, proj)


def _hgrn_kernel(q_ref, f_ref, i_ref, g_ref, lb_ref, nw_ref, o_ref, st_ref, *, n_sub):
    @pl.when(pl.program_id(2) == 0)
    def _():
        st_ref[...] = jnp.zeros_like(st_ref)

    c = B_CHUNK
    row = lax.broadcasted_iota(i32, (c, B_KEY_DIM), 0)
    r2 = lax.broadcasted_iota(i32, (c, c), 0)
    c2 = lax.broadcasted_iota(i32, (c, c), 1)
    lb = lb_ref[0]
    nw = nw_ref[...]

    def chunk(j, carry):
        r0 = pl.multiple_of(j * c, c)
        rows = pl.ds(r0, c)
        f = lb + (1.0 - lb) * jax.nn.sigmoid(f_ref[rows, :])
        g = jnp.log(f)
        kin = 1.0 - f
        q = q_ref[rows, :]
        qf = q * jax.nn.sigmoid(q) * (B_KEY_DIM ** -0.5)
        iv = i_ref[rows, :].astype(bf16)

        b = g
        d = 1
        while d < c:
            b = b + jnp.where(row >= d, pltpu.roll(b, d, 0), 0.0)
            d *= 2

        scores = jnp.where(r2 == c2, jnp.sum(qf * kin, axis=-1, keepdims=True), 0.0)
        bm = jnp.where(row >= 1, pltpu.roll(b, 1, 0), 0.0)
        m = 1
        while m < c:
            if m > 1:
                bm = jnp.where((row & (m - 1)) < m // 2, bm, pltpu.roll(bm, m // 2, 0))
            right = ((row // m) & 1) == 1
            e = jnp.where(right, b - bm, pltpu.roll(bm, c - m, 0) - b)
            xm = (jnp.where(right, qf, kin) * jnp.exp(e)).astype(bf16)
            y = lax.dot_general(xm, xm, _NT, preferred_element_type=f32)
            pair = (((r2 // m) & 1) == 1) & ((c2 // m) == (r2 // m) - 1)
            scores = scores + jnp.where(pair, y, 0.0)
            m *= 2

        bl = b[c - 1:c, :]
        st = st_ref[...]
        inter = lax.dot_general((qf * jnp.exp(b)).astype(bf16), st.astype(bf16), _NT,
                                preferred_element_type=f32)
        intra = jnp.dot(scores.astype(bf16), iv, preferred_element_type=f32)
        kl = (kin * jnp.exp(bl - b)).astype(bf16)
        st_ref[...] = st * jnp.exp(bl) + lax.dot_general(iv, kl, _TN, preferred_element_type=f32)
        o = inter + intra
        o = o * lax.rsqrt(jnp.mean(o * o, axis=-1, keepdims=True) + NORM_EPS) * nw
        gt = g_ref[rows, :]
        o_ref[rows, :] = (o * (gt * jax.nn.sigmoid(gt))).astype(o_ref.dtype)
        return carry

    lax.fori_loop(0, n_sub, chunk, 0)


def _hgrn2(proj, lb, onorm_w, batch, seq, rows=512):
    n_sub = rows // B_CHUNK
    nr = seq // rows
    blk = lambda c0: pl.BlockSpec((rows, B_KEY_DIM), lambda b, h, r: (b * nr + r, c0 // LANES + h))
    return pl.pallas_call(
        functools.partial(_hgrn_kernel, n_sub=n_sub),
        grid=(batch, B_HEADS, nr),
        in_specs=[blk(AB_BQ0), blk(AB_BF0), blk(AB_BI0), blk(AB_BG0),
                  pl.BlockSpec((1, 1, B_KEY_DIM), lambda b, h, r: (h, 0, 0)),
                  pl.BlockSpec((1, B_VAL_DIM), lambda b, h, r: (0, 0))],
        out_specs=pl.BlockSpec((rows, B_VAL_DIM), lambda b, h, r: (b * nr + r, h)),
        out_shape=jax.ShapeDtypeStruct((batch * seq, B_HEADS * B_VAL_DIM), bf16),
        scratch_shapes=[pltpu.VMEM((B_VAL_DIM, B_KEY_DIM), f32)],
        compiler_params=_params(("parallel", "parallel", "arbitrary")),
        name="hgrn2",
    )(proj, proj, proj, proj, lb, onorm_w)


def _proj_res_kernel(*refs, n_lhs):
    lhs, ws = refs[:n_lhs], refs[n_lhs:2 * n_lhs]
    x_ref, g_ref, o_ref = refs[2 * n_lhs:]
    acc = jnp.dot(lhs[0][...], ws[0][...], preferred_element_type=f32)
    for a, w in zip(lhs[1:], ws[1:]):
        acc = acc + jnp.dot(a[...], w[...], preferred_element_type=f32)
    o_ref[...] = x_ref[...] + g_ref[0] * acc


def _proj_residual(lhs_list, w, x2, gate, seq, tm=512):
    m, d = x2.shape
    kp = lhs_list[0].shape[1]
    n_lhs = len(lhs_list)
    tpb = seq // tm
    row = lambda i: (i, 0)
    in_specs = ([pl.BlockSpec((tm, kp), row) for _ in lhs_list]
                + [pl.BlockSpec((kp, d), (lambda p: (lambda i: (p, 0)))(p)) for p in range(n_lhs)]
                + [pl.BlockSpec((tm, d), row), pl.BlockSpec((1, 1, d), lambda i: (i // tpb, 0, 0))])
    return pl.pallas_call(
        functools.partial(_proj_res_kernel, n_lhs=n_lhs),
        grid=(m // tm,),
        in_specs=in_specs,
        out_specs=pl.BlockSpec((tm, d), row),
        out_shape=jax.ShapeDtypeStruct((m, d), f32),
        compiler_params=_params(("parallel",)),
        name="out_proj_residual",
    )(*lhs_list, *([w] * n_lhs), x2, gate)


def _router_kernel(x_ref, nw_ref, sc_ref, sh_ref, rw_ref, rb_ref,
                   h_ref, eid_ref, gate_ref, rank_ref, cnt_ref, carry_ref):
    @pl.when(pl.program_id(0) == 0)
    def _():
        carry_ref[...] = jnp.zeros_like(carry_ref)

    h = _norm_mod(x_ref[...], nw_ref[...], sc_ref[0], sh_ref[0])
    h_ref[...] = h
    tm = h.shape[0]
    logits = jnp.dot(h, rw_ref[...], preferred_element_type=f32,
                     precision=lax.Precision.HIGHEST) + rb_ref[...]
    lane = lax.broadcasted_iota(i32, (tm, LANES), 1)
    work = jnp.where(lane < N_EXPERTS, logits, -jnp.inf)
    hots, vals, ids = [], [], []
    for _ in range(TOP_K):
        mx = jnp.max(work, axis=-1, keepdims=True)
        idx = jnp.min(jnp.where(work == mx, lane, LANES), axis=-1, keepdims=True)
        hot = lane == idx
        work = jnp.where(hot, -jnp.inf, work)
        hots.append(hot)
        vals.append(mx)
        ids.append(idx)
    es = [jnp.exp(v - vals[0]) for v in vals]
    denom = es[0] + es[1] + es[2] + es[3]
    member = jnp.zeros((tm, LANES), f32)
    for hot in hots:
        member = member + jnp.where(hot, 1.0, 0.0)
    r2 = lax.broadcasted_iota(i32, (tm, tm), 0)
    c2 = lax.broadcasted_iota(i32, (tm, tm), 1)
    before = jnp.where(c2 < r2, 1.0, 0.0).astype(bf16)
    prior = jnp.dot(before, member.astype(bf16), preferred_element_type=f32) + carry_ref[...]
    eid = jnp.zeros((tm, LANES), i32)
    gate = jnp.zeros((tm, LANES), f32)
    rank = jnp.zeros((tm, LANES), i32)
    for k in range(TOP_K):
        rk = jnp.sum(jnp.where(hots[k], prior, 0.0), axis=-1, keepdims=True)
        eid = jnp.where(lane == k, ids[k], eid)
        gate = jnp.where(lane == k, es[k] / denom, gate)
        rank = jnp.where(lane == k, rk.astype(i32), rank)
    eid_ref[...] = eid
    gate_ref[...] = gate
    rank_ref[...] = rank
    carry_ref[...] = carry_ref[...] + jnp.sum(member, axis=0, keepdims=True)
    cnt_ref[...] = carry_ref[...]


def _router(x2, nw, sc, sh, rw, rb, seq, tm=512):
    m, d = x2.shape
    tpb = seq // tm
    row = lambda i: (i, 0)
    per_b = lambda i: (i // tpb, 0, 0)
    fixed = lambda i: (0, 0)
    return pl.pallas_call(
        _router_kernel,
        grid=(m // tm,),
        in_specs=[pl.BlockSpec((tm, d), row),
                  pl.BlockSpec((1, d), fixed),
                  pl.BlockSpec((1, 1, d), per_b),
                  pl.BlockSpec((1, 1, d), per_b),
                  pl.BlockSpec((d, LANES), fixed),
                  pl.BlockSpec((1, LANES), fixed)],
        out_specs=[pl.BlockSpec((tm, d), row),
                   pl.BlockSpec((tm, LANES), row),
                   pl.BlockSpec((tm, LANES), row),
                   pl.BlockSpec((tm, LANES), row),
                   pl.BlockSpec((1, LANES), fixed)],
        out_shape=[jax.ShapeDtypeStruct((m, d), f32),
                   jax.ShapeDtypeStruct((m, LANES), i32),
                   jax.ShapeDtypeStruct((m, LANES), f32),
                   jax.ShapeDtypeStruct((m, LANES), i32),
                   jax.ShapeDtypeStruct((1, LANES), f32)],
        scratch_shapes=[pltpu.VMEM((1, LANES), f32)],
        compiler_params=_params(("arbitrary",)),
        name="ffn_norm_router",
    )(x2, nw, sc, sh, rw, rb)


def _dispatch_kernel(pos_ref, h_ref, zin_ref, out_ref, sem):
    del zin_ref
    tm = h_ref.shape[0]

    def copy(r, k):
        return pltpu.make_async_copy(h_ref.at[pl.ds(r, 1), :],
                                     out_ref.at[pl.ds(pos_ref[TOP_K * r + k], 1), :], sem)

    def issue(r, carry):
        for k in range(TOP_K):
            copy(r, k).start()
        return carry

    def drain(r, carry):
        for k in range(TOP_K):
            copy(r, k).wait()
        return carry

    lax.fori_loop(0, tm, issue, 0)
    lax.fori_loop(0, tm, drain, 0)


def _dispatch(h2, pos_flat, n_rows, tm=512):
    m, d = h2.shape
    zeros = jnp.zeros((n_rows, d), f32)
    return pl.pallas_call(
        _dispatch_kernel,
        grid=(m // tm,),
        in_specs=[pl.BlockSpec((tm * TOP_K,), lambda i: (i,), memory_space=pltpu.SMEM),
                  pl.BlockSpec((tm, d), lambda i: (i, 0)),
                  pl.BlockSpec(memory_space=pl.ANY)],
        out_specs=pl.BlockSpec(memory_space=pl.ANY),
        out_shape=jax.ShapeDtypeStruct((n_rows, d), f32),
        scratch_shapes=[pltpu.SemaphoreType.DMA(())],
        input_output_aliases={2: 0},
        compiler_params=_params(("arbitrary",)),
        name="moe_dispatch",
    )(pos_flat, h2, zeros)


def _expert_kernel(te_ref, nu_ref, x_ref, w1_ref, b1_ref, w2_ref, b2_ref, o_ref, w1b_ref, w2b_ref):
    i = pl.program_id(0)
    e = te_ref[i]
    e_prev = te_ref[jnp.maximum(i - 1, 0)]

    @pl.when(jnp.logical_or(i == 0, e != e_prev))
    def _():
        w1b_ref[...] = w1_ref[0, 0].astype(bf16)
        w2b_ref[...] = w2_ref[0, 0].astype(bf16)

    @pl.when(i < nu_ref[0])
    def _():
        x = x_ref[...].astype(bf16)
        gu = jnp.dot(x, w1b_ref[...], preferred_element_type=f32) + b1_ref[0, 0]
        g = jnp.minimum(gu[:, :D_FF], SWIGLU_LIMIT)
        u = jnp.clip(gu[:, D_FF:], -SWIGLU_LIMIT, SWIGLU_LIMIT)
        act = g * jax.nn.sigmoid(SWIGLU_ALPHA * g) * (u + 1.0)
        o_ref[...] = jnp.dot(act.astype(bf16), w2b_ref[...], preferred_element_type=f32) + b2_ref[0, 0]

    @pl.when(i >= nu_ref[0])
    def _():
        o_ref[...] = jnp.zeros_like(o_ref)


def _expert_ffn(xs, tile_expert, n_used, w1, b1, w2, b2, layer):
    n_rows, d = xs.shape
    tm = EXPERT_TILE
    by_e = lambda i, te, nu: (layer, te[i], 0, 0)
    return pl.pallas_call(
        _expert_kernel,
        grid_spec=pltpu.PrefetchScalarGridSpec(
            num_scalar_prefetch=2,
            grid=(n_rows // tm,),
            in_specs=[pl.BlockSpec((tm, d), lambda i, te, nu: (i, 0)),
                      pl.BlockSpec((1, 1, d, 2 * D_FF), by_e),
                      pl.BlockSpec((1, 1, 1, 2 * D_FF), by_e),
                      pl.BlockSpec((1, 1, D_FF, d), by_e),
                      pl.BlockSpec((1, 1, 1, d), by_e)],
            out_specs=pl.BlockSpec((tm, d), lambda i, te, nu: (i, 0)),
            scratch_shapes=[pltpu.VMEM((d, 2 * D_FF), bf16), pltpu.VMEM((D_FF, d), bf16)]),
        out_shape=jax.ShapeDtypeStruct((n_rows, d), f32),
        compiler_params=_params(("arbitrary",)),
        name="expert_ffn",
    )(tile_expert, n_used, xs, w1, b1, w2, b2)


def _combine_kernel(pos_ref, ys_ref, gate_ref, x_ref, g2_ref, fw_ref, o_ref, buf_ref, sem, *, final_norm):
    tm = x_ref.shape[0]

    def copy(r, k):
        return pltpu.make_async_copy(ys_ref.at[pl.ds(pos_ref[TOP_K * r + k], 1), :],
                                     buf_ref.at[k, pl.ds(r, 1), :], sem)

    def issue(r, carry):
        for k in range(TOP_K):
            copy(r, k).start()
        return carry

    def drain(r, carry):
        for k in range(TOP_K):
            copy(r, k).wait()
        return carry

    lax.fori_loop(0, tm, issue, 0)
    lax.fori_loop(0, tm, drain, 0)
    gates = gate_ref[...]
    acc = gates[:, 0:1] * buf_ref[0]
    for k in range(1, TOP_K):
        acc = acc + gates[:, k:k + 1] * buf_ref[k]
    y = x_ref[...] + g2_ref[0] * acc
    if final_norm:
        y = y * lax.rsqrt(jnp.mean(y * y, axis=-1, keepdims=True) + NORM_EPS) * fw_ref[...]
    o_ref[...] = y


def _combine(ys, pos_flat, gates, x2, g2, final_w, seq, final_norm, tm=128):
    m, d = x2.shape
    tpb = seq // tm
    row = lambda i: (i, 0)
    return pl.pallas_call(
        functools.partial(_combine_kernel, final_norm=final_norm),
        grid=(m // tm,),
        in_specs=[pl.BlockSpec((tm * TOP_K,), lambda i: (i,), memory_space=pltpu.SMEM),
                  pl.BlockSpec(memory_space=pl.ANY),
                  pl.BlockSpec((tm, LANES), row),
                  pl.BlockSpec((tm, d), row),
                  pl.BlockSpec((1, 1, d), lambda i: (i // tpb, 0, 0)),
                  pl.BlockSpec((1, d), lambda i: (0, 0))],
        out_specs=pl.BlockSpec((tm, d), row),
        out_shape=jax.ShapeDtypeStruct((m, d), f32),
        scratch_shapes=[pltpu.VMEM((TOP_K, tm, d), f32), pltpu.SemaphoreType.DMA(())],
        compiler_params=_params(("arbitrary",)),
        name="moe_combine",
    )(pos_flat, ys, gates, x2, g2, final_w)


def _routed_ffn(x2, nw, sc, sh, g2, rw, rb, w1, b1, w2, b2, layer, final_w, seq, final_norm):
    m, d = x2.shape
    rw_pad = jnp.pad(rw, ((0, 0), (0, LANES - N_EXPERTS)))
    rb_pad = jnp.pad(rb, (0, LANES - N_EXPERTS)).reshape(1, LANES)
    h2, eid, gates, rank, cnt = _router(x2, nw, sc, sh, rw_pad, rb_pad, seq)
    tm = EXPERT_TILE
    n_tiles = (m * TOP_K) // tm + N_EXPERTS
    counts = cnt[0, :N_EXPERTS].astype(i32)
    padded = ((counts + tm - 1) // tm) * tm
    gend = jnp.cumsum(padded)
    gstart = gend - padded
    n_used = (gend[-1] // tm).astype(i32).reshape(1)
    tile_start = jnp.arange(n_tiles, dtype=i32) * tm
    tile_expert = jnp.sum((gend[None, :] <= tile_start[:, None]).astype(i32), axis=1)
    tile_expert = jnp.minimum(tile_expert, N_EXPERTS - 1).astype(i32)
    pos = gstart[eid[:, :TOP_K]] + rank[:, :TOP_K]
    pos_flat = pos.reshape(-1).astype(i32)
    xs = _dispatch(h2, pos_flat, n_tiles * tm)
    depth = w1.shape[0]
    ys = _expert_ffn(xs, tile_expert, n_used, w1, b1.reshape(depth, N_EXPERTS, 1, 2 * D_FF),
                     w2, b2.reshape(depth, N_EXPERTS, 1, d), layer)
    return _combine(ys, pos_flat, gates, x2, g2, final_w, seq, final_norm)


def _dsa_prep_kernel(y_ref, qn_ref, wuq_ref, wiq_ref, lnw_ref, lnb_ref, c_ref, s1_ref, s2_ref,
                     q_ref, k_ref, v_ref, qi_ref, ki_ref, wi_ref):
    y = y_ref[...]
    c, s1, s2 = c_ref[...], s1_ref[...], s2_ref[...]
    cq = y[:, :C_Q_RANK]
    cq = (cq * lax.rsqrt(jnp.mean(cq * cq, axis=-1, keepdims=True) + NORM_EPS) * qn_ref[...]).astype(bf16)
    q = jnp.dot(cq, wuq_ref[...], preferred_element_type=f32)
    for j in range(q.shape[1] // LANES):
        sl = slice(j * LANES, (j + 1) * LANES)
        q_ref[sl, :] = (_rot(q[:, sl], c, s1, s2) * (HEAD_DIM ** -0.5)).T.astype(bf16)
    qi = jnp.dot(cq, wiq_ref[...], preferred_element_type=f32)
    for j in range(qi.shape[1] // LANES):
        sl = slice(j * LANES, (j + 1) * LANES)
        qi_ref[:, sl] = (_rot(qi[:, sl], c, s1, s2) * (IDX_DIM ** -0.5)).astype(bf16)
    k_ref[...] = y[:, C_K0:C_V0].astype(bf16)
    ones_col = jnp.where(lax.broadcasted_iota(i32, (y.shape[0], HEAD_DIM), 1) == 0, 1.0, 0.0)
    for g in range(C_KV_HEADS):
        vg = jnp.concatenate([y[:, C_V0 + g * HEAD_DIM:C_V0 + (g + 1) * HEAD_DIM], ones_col], axis=1)
        v_ref[g * LANES:(g + 1) * LANES, :] = vg.T.astype(bf16)
    t6 = y[:, C_IK0:C_IK0 + LANES]
    ik = t6[:, :IDX_DIM]
    mu = jnp.mean(ik, axis=-1, keepdims=True)
    var = jnp.mean(jnp.square(ik - mu), axis=-1, keepdims=True)
    ln = (ik - mu) * lax.rsqrt(var + NORM_EPS) * lnw_ref[...] + lnb_ref[...]
    ln = jnp.concatenate([ln, jnp.zeros_like(ln)], axis=1)
    ki_ref[...] = _rot(ln, c, s1, s2)[:, :IDX_DIM].astype(bf16)
    wi_ref[...] = t6[:, IDX_DIM:IDX_DIM + IDX_HEADS] * (IDX_HEADS ** -0.5)


def _dsa_prep(y, qn, wuq, wiq, lnw, lnb, tabs, tm=256):
    m = y.shape[0]
    row = lambda i: (i, 0)
    fixed = lambda i: (0, 0)
    kvw = C_KV_HEADS * HEAD_DIM
    return pl.pallas_call(
        _dsa_prep_kernel,
        grid=(m // tm,),
        in_specs=[pl.BlockSpec((tm, C_IN_PAD), row),
                  pl.BlockSpec((1, C_Q_RANK), fixed),
                  pl.BlockSpec(wuq.shape, fixed),
                  pl.BlockSpec(wiq.shape, fixed),
                  pl.BlockSpec((1, IDX_DIM), fixed),
                  pl.BlockSpec((1, IDX_DIM), fixed),
                  pl.BlockSpec((tm, LANES), row),
                  pl.BlockSpec((tm, LANES), row),
                  pl.BlockSpec((tm, LANES), row)],
        out_specs=[pl.BlockSpec((C_HEADS * HEAD_DIM, tm), lambda i: (0, i)),
                   pl.BlockSpec((tm, kvw), row),
                   pl.BlockSpec((C_KV_HEADS * LANES, tm), lambda i: (0, i)),
                   pl.BlockSpec((tm, IDX_HEADS * IDX_DIM), row),
                   pl.BlockSpec((tm, IDX_DIM), row),
                   pl.BlockSpec((tm, IDX_HEADS), row)],
        out_shape=[jax.ShapeDtypeStruct((C_HEADS * HEAD_DIM, m), bf16),
                   jax.ShapeDtypeStruct((m, kvw), bf16),
                   jax.ShapeDtypeStruct((C_KV_HEADS * LANES, m), bf16),
                   jax.ShapeDtypeStruct((m, IDX_HEADS * IDX_DIM), bf16),
                   jax.ShapeDtypeStruct((m, IDX_DIM), bf16),
                   jax.ShapeDtypeStruct((m, IDX_HEADS), f32)],
        compiler_params=_params(("parallel",)),
        name="dsa_prep",
    )(y, qn, wuq, wiq, lnw, lnb, *tabs)


def _dsa_select_kernel(qi_ref, wi_ref, ki_ref, mask_ref, key_ref, *, seq, n_sel):
    i = pl.program_id(1)
    qb, kc = C_QBLOCK, KEY_CHUNK
    nkc = (i * qb + qb + kc - 1) // kc
    qs = jnp.concatenate([qi_ref[:, h * IDX_DIM:(h + 1) * IDX_DIM] for h in range(IDX_HEADS)], axis=0)
    wi = wi_ref[...]
    qpos = i * qb + lax.broadcasted_iota(i32, (qb, kc), 0)
    kiota = lax.broadcasted_iota(i32, (qb, kc), 1)

    def score_chunk(c, carry):
        k0 = pl.multiple_of(c * kc, kc)
        lg = lax.dot_general(qs, ki_ref[pl.ds(k0, kc), :], _NT, preferred_element_type=f32)
        sc = wi[:, 0:1] * jnp.maximum(lg[0:qb], 0.0)
        for h in range(1, IDX_HEADS):
            sc = sc + wi[:, h:h + 1] * jnp.maximum(lg[h * qb:(h + 1) * qb], 0.0)
        sc = jnp.where(sc == 0.0, 0.0, sc)
        sc = jnp.where(k0 + kiota <= qpos, sc, -jnp.inf)
        bits = pltpu.bitcast(sc, i32)
        key_ref[:, pl.ds(k0, kc)] = bits ^ ((bits >> 31) & 0x7FFFFFFF)
        return carry

    lax.fori_loop(0, nkc, score_chunk, 0)

    def count(pred):
        def body(c, acc):
            k0 = pl.multiple_of(c * kc, kc)
            hit = jnp.where(pred(key_ref[:, pl.ds(k0, kc)]), 1, 0)
            for j in range(kc // LANES):
                acc = acc + hit[:, j * LANES:(j + 1) * LANES]
            return acc
        acc = lax.fori_loop(0, nkc, body, jnp.zeros((qb, LANES), i32))
        return jnp.sum(acc, axis=-1, keepdims=True)

    thr = jnp.where(count(lambda k: k >= 0) >= n_sel, 0, INT_MIN).astype(i32)

    def bit_step(bi, thr):
        cand = thr | jnp.left_shift(jnp.int32(1), 30 - bi)
        return jnp.where(count(lambda k: k >= cand) >= n_sel, cand, thr)

    thr = lax.fori_loop(0, 31, bit_step, thr)
    allowed = (n_sel - count(lambda k: k > thr)).astype(f32)
    r2 = lax.broadcasted_iota(i32, (kc, kc), 0)
    c2 = lax.broadcasted_iota(i32, (kc, kc), 1)
    upto = jnp.where(r2 <= c2, 1.0, 0.0).astype(bf16)

    def out_chunk(c, ties_before):
        k0 = pl.multiple_of(c * kc, kc)
        key = key_ref[:, pl.ds(k0, kc)]
        eq = jnp.where(key == thr, 1.0, 0.0)
        tie_rank = jnp.dot(eq.astype(bf16), upto, preferred_element_type=f32) + ties_before
        tie_ok = (key == thr) & (tie_rank <= allowed)
        sel = (k0 + kiota <= qpos) & ((key > thr) | tie_ok)
        mask_ref[0, pl.ds(k0, kc), :] = jnp.where(sel, 1.0, 0.0).T.astype(i32).astype(jnp.int8)
        return ties_before + jnp.sum(eq, axis=-1, keepdims=True)

    lax.fori_loop(0, nkc, out_chunk, jnp.zeros((qb, 1), f32))

    def zero_chunk(c, carry):
        k0 = pl.multiple_of(c * kc, kc)
        mask_ref[0, pl.ds(k0, kc), :] = jnp.zeros((kc, qb), jnp.int8)
        return carry

    lax.fori_loop(nkc, seq // kc, zero_chunk, 0)


def _dsa_select(qi, wi, ki, batch, seq):
    nq = seq // C_QBLOCK
    n_sel = min(IDX_TOPK, seq // 4)
    return pl.pallas_call(
        functools.partial(_dsa_select_kernel, seq=seq, n_sel=n_sel),
        grid=(batch, nq),
        in_specs=[pl.BlockSpec((C_QBLOCK, IDX_HEADS * IDX_DIM), lambda b, i: (b * nq + i, 0)),
                  pl.BlockSpec((C_QBLOCK, IDX_HEADS), lambda b, i: (b * nq + i, 0)),
                  pl.BlockSpec((seq, IDX_DIM), lambda b, i: (b, 0))],
        out_specs=pl.BlockSpec((1, seq, C_QBLOCK), lambda b, i: (b, 0, i)),
        out_shape=jax.ShapeDtypeStruct((batch, seq, seq), jnp.int8),
        scratch_shapes=[pltpu.VMEM((C_QBLOCK, seq), i32)],
        compiler_params=_params(("parallel", "parallel")),
        name="dsa_select",
    )(qi, wi, ki)


def _dsa_attn_kernel(qt_ref, k_ref, vt_ref, mask_ref, o_ref, qx_ref, m_ref, acc_ref):
    i = pl.program_id(1)
    qb, kc = C_QBLOCK, KEY_CHUNK
    g_sz = C_HEADS // C_KV_HEADS
    nkc = (i * qb + qb + kc - 1) // kc
    qx_ref[...] = jnp.zeros_like(qx_ref)
    for g in range(C_KV_HEADS):
        for hl in range(g_sz):
            h = g * g_sz + hl
            qx_ref[g, g * HEAD_DIM:(g + 1) * HEAD_DIM, hl * qb:(hl + 1) * qb] = qt_ref[h * HEAD_DIM:(h + 1) * HEAD_DIM, :]
    m_ref[...] = jnp.full_like(m_ref, NEG_BIG)
    acc_ref[...] = jnp.zeros_like(acc_ref)

    def body(c, carry):
        k0 = pl.multiple_of(c * kc, kc)
        kk = k_ref[pl.ds(k0, kc), :]
        bias = jnp.where(mask_ref[0, pl.ds(k0, kc), :].astype(i32) != 0, 0.0, -jnp.inf)
        bias = jnp.concatenate([bias] * g_sz, axis=1)
        for g in range(C_KV_HEADS):
            s = jnp.dot(kk, qx_ref[g], preferred_element_type=f32) + bias
            m_old = m_ref[g]
            m_new = jnp.maximum(m_old, jnp.max(s, axis=0, keepdims=True))
            p = jnp.exp(s - m_new).astype(bf16)
            acc_ref[g] = (jnp.exp(m_old - m_new) * acc_ref[g]
                          + jnp.dot(vt_ref[g * LANES:(g + 1) * LANES, pl.ds(k0, kc)], p, preferred_element_type=f32))
            m_ref[g] = m_new
        return carry

    lax.fori_loop(0, nkc, body, 0)
    outs = []
    for g in range(C_KV_HEADS):
        a = acc_ref[g]
        o = a[:HEAD_DIM] / a[HEAD_DIM:HEAD_DIM + 1]
        for hl in range(g_sz):
            outs.append(o[:, hl * qb:(hl + 1) * qb].T)
    o_ref[...] = jnp.concatenate(outs, axis=1).astype(o_ref.dtype)


def _dsa_attention(qt, k, vt, mask, batch, seq):
    nq = seq // C_QBLOCK
    g_cols = (C_HEADS // C_KV_HEADS) * C_QBLOCK
    kvw = C_KV_HEADS * HEAD_DIM
    return pl.pallas_call(
        _dsa_attn_kernel,
        grid=(batch, nq),
        in_specs=[pl.BlockSpec((C_HEADS * HEAD_DIM, C_QBLOCK), lambda b, i: (0, b * nq + i)),
                  pl.BlockSpec((seq, kvw), lambda b, i: (b, 0)),
                  pl.BlockSpec((C_KV_HEADS * LANES, seq), lambda b, i: (0, b)),
                  pl.BlockSpec((1, seq, C_QBLOCK), lambda b, i: (b, 0, i))],
        out_specs=pl.BlockSpec((C_QBLOCK, C_HEADS * HEAD_DIM), lambda b, i: (b * nq + i, 0)),
        out_shape=jax.ShapeDtypeStruct((batch * seq, C_HEADS * HEAD_DIM), bf16),
        scratch_shapes=[pltpu.VMEM((C_KV_HEADS, kvw, g_cols), bf16),
                        pltpu.VMEM((C_KV_HEADS, 1, g_cols), f32),
                        pltpu.VMEM((C_KV_HEADS, LANES, g_cols), f32)],
        compiler_params=_params(("parallel", "parallel")),
        name="dsa_attention",
    )(qt, k, vt, mask)


def _rotary_tables(positions, rot_dim):
    half = rot_dim // 2
    inv_freq = ROPE_THETA ** (-jnp.arange(0, rot_dim, 2, dtype=f32) / rot_dim)
    ang = positions.astype(f32).reshape(-1, 1) * inv_freq
    cos, sin = jnp.cos(ang), jnp.sin(ang)
    n = ang.shape[0]
    rest = HEAD_DIM - 2 * half
    c = jnp.concatenate([cos, cos, jnp.ones((n, rest), f32)], axis=-1)
    s1 = jnp.concatenate([-sin, jnp.zeros((n, half + rest), f32)], axis=-1)
    s2 = jnp.concatenate([jnp.zeros((n, half), f32), sin, jnp.zeros((n, rest), f32)], axis=-1)
    rep = LANES // HEAD_DIM
    return tuple(jnp.tile(t, (1, rep)) for t in (c, s1, s2))


def kernel(x, c, positions, mod_w, mod_b, norm_mix_w, norm_ffn_w, ab_w_in, ab_w_out, a_sinks, b_lb_logits,
           b_onorm_w, c_w_in, c_q_norm_w, c_w_uq, c_w_iq, c_ik_norm_w, c_ik_norm_b, c_w_out, router_w,
           router_b, moe_w1, moe_b1, moe_w2, moe_b2, final_norm_w):
    batch, seq, d = x.shape
    depth = mod_w.shape[0]
    tabs = _rotary_tables(positions, ROT_DIM)
    tabs_idx = _rotary_tables(positions, IDX_ROT_DIM)
    lower_bounds = jnp.cumsum(jax.nn.softmax(b_lb_logits.astype(f32), axis=0), axis=0)
    cond = jnp.pad(jax.nn.silu(c), ((0, 8 - batch), (0, 0)))
    x2 = x.reshape(batch * seq, d)
    final_w = final_norm_w.reshape(1, d)
    for layer in range(depth):
        mod = _dense(cond, mod_w[layer], mod_b[layer].reshape(1, -1))[:batch]
        sh1, sc1, g1, sh2, sc2, g2 = [t.reshape(batch, 1, d) for t in jnp.split(mod, 6, axis=-1)]
        nw = norm_mix_w[layer].reshape(1, d)
        j = layer // 2
        if layer % 2 == 0:
            proj = _norm_mod_matmul(x2, nw, sc1, sh1, ab_w_in[j].astype(bf16), tabs, seq,
                                    rot_lo=AB_Q0 // LANES, rot_hi=AB_V0 // LANES)
            out_a = _swa_attention(proj, a_sinks[j], batch, seq)
            out_b = _hgrn2(proj, lower_bounds[j].reshape(B_HEADS, 1, B_KEY_DIM),
                           b_onorm_w[j].reshape(1, B_VAL_DIM), batch, seq)
            x2 = _proj_residual([out_a, out_b], ab_w_out[j].astype(bf16), x2, g1, seq)
        else:
            w_in = jnp.pad(c_w_in[j], ((0, 0), (0, C_IN_PAD - c_w_in.shape[-1]))).astype(bf16)
            y = _norm_mod_matmul(x2, nw, sc1, sh1, w_in, tabs, seq,
                                 rot_lo=C_K0 // LANES, rot_hi=C_V0 // LANES)
            q, k, v, qi, ki, wi = _dsa_prep(y, c_q_norm_w[j].reshape(1, -1), c_w_uq[j].astype(bf16),
                                            c_w_iq[j].astype(bf16), c_ik_norm_w[j].reshape(1, -1),
                                            c_ik_norm_b[j].reshape(1, -1), tabs_idx)
            mask = _dsa_select(qi, wi, ki, batch, seq)
            o = _dsa_attention(q, k, v, mask, batch, seq)
            x2 = _proj_residual([o], c_w_out[j].astype(bf16), x2, g1, seq)
        x2 = _routed_ffn(x2, norm_ffn_w[layer].reshape(1, d), sc2, sh2, g2, router_w[layer], router_b[layer],
                         moe_w1, moe_b1, moe_w2, moe_b2, layer, final_w, seq,
                         final_norm=(layer == depth - 1))
    return x2.reshape(batch, seq, d)
```

```python
import functools

import jax
import jax.numpy as jnp
from jax import lax
from jax.experimental import pallas as pl
from jax.experimental.pallas import tpu as pltpu

f32 = jnp.float32
bf16 = jnp.bfloat16
i32 = jnp.int32

D_MODEL = 1024
HEAD_DIM = 64
ROT_DIM = HEAD_DIM // 4
ROPE_THETA = 500000.0
NORM_EPS = 1e-5
A_HEADS = 8
A_KV_HEADS = 2
WINDOW = 128
B_HEADS = 4
B_KEY_DIM = 128
B_VAL_DIM = 128
B_CHUNK = 64
C_HEADS = 16
C_KV_HEADS = 4
C_Q_RANK = 256
IDX_HEADS = 8
IDX_DIM = 64
IDX_ROT_DIM = IDX_DIM // 4
IDX_TOPK = 256
C_QBLOCK = 128
N_EXPERTS = 32
TOP_K = 4
D_FF = D_MODEL
SWIGLU_LIMIT = 7.0
SWIGLU_ALPHA = 1.702

LANES = 128
AB_IN = 2560
AB_Q0, AB_K0, AB_V0 = 0, 512, 640
AB_BQ0, AB_BF0, AB_BI0, AB_BG0 = 768, 1280, 1792, 2304
C_IN_PAD = 896
C_K0, C_V0, C_IK0 = 256, 512, 768

VMEM_LIMIT = 56 * 1024 * 1024
EXPERT_TILE = 256
KEY_CHUNK = 512
ATTN_CHUNK = 512
NEG_BIG = -1e30
INT_MIN = -(2 ** 31)

_NT = (((1,), (1,)), ((), ()))
_TN = (((0,), (0,)), ((), ()))


def _params(sem, vmem=VMEM_LIMIT):
    return pltpu.CompilerParams(dimension_semantics=sem, vmem_limit_bytes=vmem)


def _rot(t, c, s1, s2):
    return t * c + pltpu.roll(t, LANES - ROT_DIM // 2, 1) * s1 + pltpu.roll(t, ROT_DIM // 2, 1) * s2


def _norm_mod(x, nw, sc, sh):
    ms = jnp.mean(x * x, axis=-1, keepdims=True)
    return (x * lax.rsqrt(ms + NORM_EPS) * nw) * (1.0 + sc) + sh


def _dense_kernel(x_ref, w_ref, b_ref, o_ref):
    o_ref[...] = jnp.dot(x_ref[...], w_ref[...], preferred_element_type=f32,
                         precision=lax.Precision.HIGHEST) + b_ref[...]


def _dense(x, w, b, tn=1024):
    m, k = x.shape
    n = w.shape[1]
    return pl.pallas_call(
        _dense_kernel,
        grid=(n // tn,),
        in_specs=[pl.BlockSpec((m, k), lambda j: (0, 0)),
                  pl.BlockSpec((k, tn), lambda j: (0, j)),
                  pl.BlockSpec((1, tn), lambda j: (0, j))],
        out_specs=pl.BlockSpec((m, tn), lambda j: (0, j)),
        out_shape=jax.ShapeDtypeStruct((m, n), f32),
        compiler_params=_params(("parallel",)),
        name="adaln_dense",
    )(x, w, b)


def _nmm_kernel(x_ref, nw_ref, sc_ref, sh_ref, w_ref, c_ref, s1_ref, s2_ref, o_ref, *, rot_lo, rot_hi):
    h = _norm_mod(x_ref[...], nw_ref[...], sc_ref[0], sh_ref[0])
    y = jnp.dot(h.astype(bf16), w_ref[...], preferred_element_type=f32)
    c, s1, s2 = c_ref[...], s1_ref[...], s2_ref[...]
    for j in range(y.shape[1] // LANES):
        yc = y[:, j * LANES:(j + 1) * LANES]
        if rot_lo <= j < rot_hi:
            yc = _rot(yc, c, s1, s2)
        o_ref[:, j * LANES:(j + 1) * LANES] = yc


def _norm_mod_matmul(x2, nw, sc, sh, w, tabs, seq, rot_lo, rot_hi, tm=256):
    m, d = x2.shape
    n = w.shape[1]
    tpb = seq // tm
    row = lambda i: (i, 0)
    per_b = lambda i: (i // tpb, 0, 0)
    return pl.pallas_call(
        functools.partial(_nmm_kernel, rot_lo=rot_lo, rot_hi=rot_hi),
        grid=(m // tm,),
        in_specs=[pl.BlockSpec((tm, d), row),
                  pl.BlockSpec((1, d), lambda i: (0, 0)),
                  pl.BlockSpec((1, 1, d), per_b),
                  pl.BlockSpec((1, 1, d), per_b),
                  pl.BlockSpec((d, n), lambda i: (0, 0)),
                  pl.BlockSpec((tm, LANES), row),
                  pl.BlockSpec((tm, LANES), row),
                  pl.BlockSpec((tm, LANES), row)],
        out_specs=pl.BlockSpec((tm, n), row),
        out_shape=jax.ShapeDtypeStruct((m, n), f32),
        compiler_params=_params(("parallel",)),
        name="norm_mod_proj",
    )(x2, nw, sc, sh, w, *tabs)


def _swa_kernel(sink_ref, q_ref, kp_ref, kc_ref, vp_ref, vc_ref, o_ref):
    i = pl.program_id(1)
    w = WINDOW
    g_sz = A_HEADS // A_KV_HEADS
    q = q_ref[...]
    row = lax.broadcasted_iota(i32, (g_sz * w, 2 * w), 0)
    col = lax.broadcasted_iota(i32, (g_sz * w, 2 * w), 1)
    dist = (row & (w - 1)) + w - col
    lo = jnp.where(i > 0, 0, w)
    valid = (dist >= 0) & (dist < w) & (col >= lo)
    hrow = lax.broadcasted_iota(i32, (g_sz * w, 1), 0) // w
    outs = []
    for g in range(A_KV_HEADS):
        cs = slice(g * HEAD_DIM, (g + 1) * HEAD_DIM)
        kk = jnp.concatenate([kp_ref[:, cs], kc_ref[:, cs]], axis=0).astype(bf16)
        vv = jnp.concatenate([vp_ref[:, cs], vc_ref[:, cs]], axis=0).astype(bf16)
        qg = jnp.concatenate([q[:, (g * g_sz + hl) * HEAD_DIM:(g * g_sz + hl + 1) * HEAD_DIM]
                              for hl in range(g_sz)], axis=0).astype(bf16)
        s = lax.dot_general(qg, kk, _NT, preferred_element_type=f32) * (HEAD_DIM ** -0.5)
        s = jnp.where(valid, s, -jnp.inf)
        sink = jnp.zeros((g_sz * w, 1), f32)
        for hl in range(g_sz):
            sink = jnp.where(hrow == hl, sink_ref[g * g_sz + hl], sink)
        mx = jnp.maximum(jnp.max(s, axis=-1, keepdims=True), sink)
        p = jnp.exp(s - mx)
        p = p / (jnp.sum(p, axis=-1, keepdims=True) + jnp.exp(sink - mx))
        o = jnp.dot(p.astype(bf16), vv, preferred_element_type=f32)
        for hl in range(g_sz):
            outs.append(o[hl * w:(hl + 1) * w])
    o_ref[...] = jnp.concatenate(outs, axis=1).astype(o_ref.dtype)


def _swa_attention(proj, sinks, batch, seq):
    nb = seq // WINDOW
    cur = lambda cb: (lambda b, i: (b * nb + i, cb))
    prev = lambda cb: (lambda b, i: (b * nb + jnp.maximum(i - 1, 0), cb))
    kcb, vcb = AB_K0 // LANES, AB_V0 // LANES
    return pl.pallas_call(
        _swa_kernel,
        grid=(batch, nb),
        in_specs=[pl.BlockSpec(memory_space=pltpu.SMEM),
                  pl.BlockSpec((WINDOW, A_HEADS * HEAD_DIM), cur(0)),
                  pl.BlockSpec((WINDOW, LANES), prev(kcb)),
                  pl.BlockSpec((WINDOW, LANES), cur(kcb)),
                  pl.BlockSpec((WINDOW, LANES), prev(vcb)),
                  pl.BlockSpec((WINDOW, LANES), cur(vcb))],
        out_specs=pl.BlockSpec((WINDOW, A_HEADS * HEAD_DIM), cur(0)),
        out_shape=jax.ShapeDtypeStruct((batch * seq, A_HEADS * HEAD_DIM), bf16),
        compiler_params=_params(("parallel", "parallel")),
        name="swa_attention",
    )(sinks, proj, proj, proj, proj, proj)


def _hgrn_kernel(q_ref, f_ref, i_ref, g_ref, lb_ref, nw_ref, o_ref, st_ref, *, n_sub):
    @pl.when(pl.program_id(2) == 0)
    def _():
        st_ref[...] = jnp.zeros_like(st_ref)

    c = B_CHUNK
    row = lax.broadcasted_iota(i32, (c, B_KEY_DIM), 0)
    r2 = lax.broadcasted_iota(i32, (c, c), 0)
    c2 = lax.broadcasted_iota(i32, (c, c), 1)
    lb = lb_ref[0]
    nw = nw_ref[...]

    def chunk(j, carry):
        r0 = pl.multiple_of(j * c, c)
        rows = pl.ds(r0, c)
        f = lb + (1.0 - lb) * jax.nn.sigmoid(f_ref[rows, :])
        g = jnp.log(f)
        kin = 1.0 - f
        q = q_ref[rows, :]
        qf = q * jax.nn.sigmoid(q) * (B_KEY_DIM ** -0.5)
        iv = i_ref[rows, :].astype(bf16)

        b = g
        d = 1
        while d < c:
            b = b + jnp.where(row >= d, pltpu.roll(b, d, 0), 0.0)
            d *= 2

        scores = jnp.where(r2 == c2, jnp.sum(qf * kin, axis=-1, keepdims=True), 0.0)
        bm = jnp.where(row >= 1, pltpu.roll(b, 1, 0), 0.0)
        m = 1
        while m < c:
            if m > 1:
                bm = jnp.where((row & (m - 1)) < m // 2, bm, pltpu.roll(bm, m // 2, 0))
            right = ((row // m) & 1) == 1
            e = jnp.where(right, b - bm, pltpu.roll(bm, c - m, 0) - b)
            xm = (jnp.where(right, qf, kin) * jnp.exp(e)).astype(bf16)
            y = lax.dot_general(xm, xm, _NT, preferred_element_type=f32)
            pair = (((r2 // m) & 1) == 1) & ((c2 // m) == (r2 // m) - 1)
            scores = scores + jnp.where(pair, y, 0.0)
            m *= 2

        bl = b[c - 1:c, :]
        st = st_ref[...]
        inter = lax.dot_general((qf * jnp.exp(b)).astype(bf16), st.astype(bf16), _NT,
                                preferred_element_type=f32)
        intra = jnp.dot(scores.astype(bf16), iv, preferred_element_type=f32)
        kl = (kin * jnp.exp(bl - b)).astype(bf16)
        st_ref[...] = st * jnp.exp(bl) + lax.dot_general(iv, kl, _TN, preferred_element_type=f32)
        o = inter + intra
        o = o * lax.rsqrt(jnp.mean(o * o, axis=-1, keepdims=True) + NORM_EPS) * nw
        gt = g_ref[rows, :]
        o_ref[rows, :] = (o * (gt * jax.nn.sigmoid(gt))).astype(o_ref.dtype)
        return carry

    lax.fori_loop(0, n_sub, chunk, 0, unroll=4)


def _hgrn2(proj, lb, onorm_w, batch, seq, rows=512):
    n_sub = rows // B_CHUNK
    nr = seq // rows
    blk = lambda c0: pl.BlockSpec((rows, B_KEY_DIM), lambda b, h, r: (b * nr + r, c0 // LANES + h))
    return pl.pallas_call(
        functools.partial(_hgrn_kernel, n_sub=n_sub),
        grid=(batch, B_HEADS, nr),
        in_specs=[blk(AB_BQ0), blk(AB_BF0), blk(AB_BI0), blk(AB_BG0),
                  pl.BlockSpec((1, 1, B_KEY_DIM), lambda b, h, r: (h, 0, 0)),
                  pl.BlockSpec((1, B_VAL_DIM), lambda b, h, r: (0, 0))],
        out_specs=pl.BlockSpec((rows, B_VAL_DIM), lambda b, h, r: (b * nr + r, h)),
        out_shape=jax.ShapeDtypeStruct((batch * seq, B_HEADS * B_VAL_DIM), bf16),
        scratch_shapes=[pltpu.VMEM((B_VAL_DIM, B_KEY_DIM), f32)],
        compiler_params=_params(("parallel", "parallel", "arbitrary")),
        name="hgrn2",
    )(proj, proj, proj, proj, lb, onorm_w)


def _proj_res_kernel(*refs, n_lhs):
    lhs, ws = refs[:n_lhs], refs[n_lhs:2 * n_lhs]
    x_ref, g_ref, o_ref = refs[2 * n_lhs:]
    acc = jnp.dot(lhs[0][...], ws[0][...], preferred_element_type=f32)
    for a, w in zip(lhs[1:], ws[1:]):
        acc = acc + jnp.dot(a[...], w[...], preferred_element_type=f32)
    o_ref[...] = x_ref[...] + g_ref[0] * acc


def _proj_residual(lhs_list, w, x2, gate, seq, tm=512):
    m, d = x2.shape
    kp = lhs_list[0].shape[1]
    n_lhs = len(lhs_list)
    tpb = seq // tm
    row = lambda i: (i, 0)
    in_specs = ([pl.BlockSpec((tm, kp), row) for _ in lhs_list]
                + [pl.BlockSpec((kp, d), (lambda p: (lambda i: (p, 0)))(p)) for p in range(n_lhs)]
                + [pl.BlockSpec((tm, d), row), pl.BlockSpec((1, 1, d), lambda i: (i // tpb, 0, 0))])
    return pl.pallas_call(
        functools.partial(_proj_res_kernel, n_lhs=n_lhs),
        grid=(m // tm,),
        in_specs=in_specs,
        out_specs=pl.BlockSpec((tm, d), row),
        out_shape=jax.ShapeDtypeStruct((m, d), f32),
        compiler_params=_params(("parallel",)),
        name="out_proj_residual",
    )(*lhs_list, *([w] * n_lhs), x2, gate)


def _router_kernel(x_ref, nw_ref, sc_ref, sh_ref, rw_ref, rb_ref,
                   h_ref, eid_ref, gate_ref, rank_ref, cnt_ref, carry_ref):
    @pl.when(pl.program_id(0) == 0)
    def _():
        carry_ref[...] = jnp.zeros_like(carry_ref)

    h = _norm_mod(x_ref[...], nw_ref[...], sc_ref[0], sh_ref[0])
    h_ref[...] = h
    tm = h.shape[0]
    logits = jnp.dot(h, rw_ref[...], preferred_element_type=f32,
                     precision=lax.Precision.HIGHEST) + rb_ref[...]
    lane = lax.broadcasted_iota(i32, (tm, LANES), 1)
    work = jnp.where(lane < N_EXPERTS, logits, -jnp.inf)
    hots, vals, ids = [], [], []
    for _ in range(TOP_K):
        mx = jnp.max(work, axis=-1, keepdims=True)
        idx = jnp.min(jnp.where(work == mx, lane, LANES), axis=-1, keepdims=True)
        hot = lane == idx
        work = jnp.where(hot, -jnp.inf, work)
        hots.append(hot)
        vals.append(mx)
        ids.append(idx)
    es = [jnp.exp(v - vals[0]) for v in vals]
    denom = es[0] + es[1] + es[2] + es[3]
    member = jnp.zeros((tm, LANES), f32)
    for hot in hots:
        member = member + jnp.where(hot, 1.0, 0.0)
    r2 = lax.broadcasted_iota(i32, (tm, tm), 0)
    c2 = lax.broadcasted_iota(i32, (tm, tm), 1)
    before = jnp.where(c2 < r2, 1.0, 0.0).astype(bf16)
    prior = jnp.dot(before, member.astype(bf16), preferred_element_type=f32) + carry_ref[...]
    eid = jnp.zeros((tm, LANES), i32)
    gate = jnp.zeros((tm, LANES), f32)
    rank = jnp.zeros((tm, LANES), i32)
    for k in range(TOP_K):
        rk = jnp.sum(jnp.where(hots[k], prior, 0.0), axis=-1, keepdims=True)
        eid = jnp.where(lane == k, ids[k], eid)
        gate = jnp.where(lane == k, es[k] / denom, gate)
        rank = jnp.where(lane == k, rk.astype(i32), rank)
    eid_ref[...] = eid
    gate_ref[...] = gate
    rank_ref[...] = rank
    carry_ref[...] = carry_ref[...] + jnp.sum(member, axis=0, keepdims=True)
    cnt_ref[...] = carry_ref[...]


def _router(x2, nw, sc, sh, rw, rb, seq, tm=512):
    m, d = x2.shape
    tpb = seq // tm
    row = lambda i: (i, 0)
    per_b = lambda i: (i // tpb, 0, 0)
    fixed = lambda i: (0, 0)
    return pl.pallas_call(
        _router_kernel,
        grid=(m // tm,),
        in_specs=[pl.BlockSpec((tm, d), row),
                  pl.BlockSpec((1, d), fixed),
                  pl.BlockSpec((1, 1, d), per_b),
                  pl.BlockSpec((1, 1, d), per_b),
                  pl.BlockSpec((d, LANES), fixed),
                  pl.BlockSpec((1, LANES), fixed)],
        out_specs=[pl.BlockSpec((tm, d), row),
                   pl.BlockSpec((tm, LANES), row),
                   pl.BlockSpec((tm, LANES), row),
                   pl.BlockSpec((tm, LANES), row),
                   pl.BlockSpec((1, LANES), fixed)],
        out_shape=[jax.ShapeDtypeStruct((m, d), f32),
                   jax.ShapeDtypeStruct((m, LANES), i32),
                   jax.ShapeDtypeStruct((m, LANES), f32),
                   jax.ShapeDtypeStruct((m, LANES), i32),
                   jax.ShapeDtypeStruct((1, LANES), f32)],
        scratch_shapes=[pltpu.VMEM((1, LANES), f32)],
        compiler_params=_params(("arbitrary",)),
        name="ffn_norm_router",
    )(x2, nw, sc, sh, rw, rb)


def _dispatch_kernel(pos_ref, h_ref, zin_ref, out_ref, sem):
    del zin_ref
    tm = h_ref.shape[0]

    def copy(r, k):
        return pltpu.make_async_copy(h_ref.at[pl.ds(r, 1), :],
                                     out_ref.at[pl.ds(pos_ref[TOP_K * r + k], 1), :], sem)

    def issue(r, carry):
        for k in range(TOP_K):
            copy(r, k).start()
        return carry

    def drain(r, carry):
        for k in range(TOP_K):
            copy(r, k).wait()
        return carry

    lax.fori_loop(0, tm, issue, 0, unroll=4)
    lax.fori_loop(0, tm, drain, 0, unroll=4)


def _dispatch(h2, pos_flat, n_rows, tm=512):
    m, d = h2.shape
    zeros = jnp.zeros((n_rows, d), f32)
    return pl.pallas_call(
        _dispatch_kernel,
        grid=(m // tm,),
        in_specs=[pl.BlockSpec((tm * TOP_K,), lambda i: (i,), memory_space=pltpu.SMEM),
                  pl.BlockSpec((tm, d), lambda i: (i, 0)),
                  pl.BlockSpec(memory_space=pl.ANY)],
        out_specs=pl.BlockSpec(memory_space=pl.ANY),
        out_shape=jax.ShapeDtypeStruct((n_rows, d), f32),
        scratch_shapes=[pltpu.SemaphoreType.DMA(())],
        input_output_aliases={2: 0},
        compiler_params=_params(("arbitrary",)),
        name="moe_dispatch",
    )(pos_flat, h2, zeros)


def _expert_kernel(te_ref, nu_ref, x_ref, w1_ref, b1_ref, w2_ref, b2_ref, o_ref, w1b_ref, w2b_ref):
    i = pl.program_id(0)
    e = te_ref[i]
    e_prev = te_ref[jnp.maximum(i - 1, 0)]

    @pl.when(jnp.logical_or(i == 0, e != e_prev))
    def _():
        w1b_ref[...] = w1_ref[0, 0].astype(bf16)
        w2b_ref[...] = w2_ref[0, 0].astype(bf16)

    @pl.when(i < nu_ref[0])
    def _():
        x = x_ref[...].astype(bf16)
        gu = jnp.dot(x, w1b_ref[...], preferred_element_type=f32) + b1_ref[0, 0]
        g = jnp.minimum(gu[:, :D_FF], SWIGLU_LIMIT)
        u = jnp.clip(gu[:, D_FF:], -SWIGLU_LIMIT, SWIGLU_LIMIT)
        act = g * jax.nn.sigmoid(SWIGLU_ALPHA * g) * (u + 1.0)
        o_ref[...] = jnp.dot(act.astype(bf16), w2b_ref[...], preferred_element_type=f32) + b2_ref[0, 0]

    @pl.when(i >= nu_ref[0])
    def _():
        o_ref[...] = jnp.zeros_like(o_ref)


def _expert_ffn(xs, tile_expert, n_used, w1, b1, w2, b2, layer):
    n_rows, d = xs.shape
    tm = EXPERT_TILE
    by_e = lambda i, te, nu: (layer, te[i], 0, 0)
    return pl.pallas_call(
        _expert_kernel,
        grid_spec=pltpu.PrefetchScalarGridSpec(
            num_scalar_prefetch=2,
            grid=(n_rows // tm,),
            in_specs=[pl.BlockSpec((tm, d), lambda i, te, nu: (i, 0)),
                      pl.BlockSpec((1, 1, d, 2 * D_FF), by_e),
                      pl.BlockSpec((1, 1, 1, 2 * D_FF), by_e),
                      pl.BlockSpec((1, 1, D_FF, d), by_e),
                      pl.BlockSpec((1, 1, 1, d), by_e)],
            out_specs=pl.BlockSpec((tm, d), lambda i, te, nu: (i, 0)),
            scratch_shapes=[pltpu.VMEM((d, 2 * D_FF), bf16), pltpu.VMEM((D_FF, d), bf16)]),
        out_shape=jax.ShapeDtypeStruct((n_rows, d), f32),
        compiler_params=_params(("arbitrary",)),
        name="expert_ffn",
    )(tile_expert, n_used, xs, w1, b1, w2, b2)


def _combine_kernel(pos_ref, ys_ref, gate_ref, x_ref, g2_ref, fw_ref, o_ref, buf_ref, sem, *, final_norm):
    tm = x_ref.shape[0]

    def copy(r, k):
        return pltpu.make_async_copy(ys_ref.at[pl.ds(pos_ref[TOP_K * r + k], 1), :],
                                     buf_ref.at[k, pl.ds(r, 1), :], sem)

    def issue(r, carry):
        for k in range(TOP_K):
            copy(r, k).start()
        return carry

    def drain(r, carry):
        for k in range(TOP_K):
            copy(r, k).wait()
        return carry

    lax.fori_loop(0, tm, issue, 0, unroll=4)
    lax.fori_loop(0, tm, drain, 0, unroll=4)
    gates = gate_ref[...]
    acc = gates[:, 0:1] * buf_ref[0]
    for k in range(1, TOP_K):
        acc = acc + gates[:, k:k + 1] * buf_ref[k]
    y = x_ref[...] + g2_ref[0] * acc
    if final_norm:
        y = y * lax.rsqrt(jnp.mean(y * y, axis=-1, keepdims=True) + NORM_EPS) * fw_ref[...]
    o_ref[...] = y


def _combine(ys, pos_flat, gates, x2, g2, final_w, seq, final_norm, tm=256):
    m, d = x2.shape
    tpb = seq // tm
    row = lambda i: (i, 0)
    return pl.pallas_call(
        functools.partial(_combine_kernel, final_norm=final_norm),
        grid=(m // tm,),
        in_specs=[pl.BlockSpec((tm * TOP_K,), lambda i: (i,), memory_space=pltpu.SMEM),
                  pl.BlockSpec(memory_space=pl.ANY),
                  pl.BlockSpec((tm, LANES), row),
                  pl.BlockSpec((tm, d), row),
                  pl.BlockSpec((1, 1, d), lambda i: (i // tpb, 0, 0)),
                  pl.BlockSpec((1, d), lambda i: (0, 0))],
        out_specs=pl.BlockSpec((tm, d), row),
        out_shape=jax.ShapeDtypeStruct((m, d), f32),
        scratch_shapes=[pltpu.VMEM((TOP_K, tm, d), f32), pltpu.SemaphoreType.DMA(())],
        compiler_params=_params(("arbitrary",)),
        name="moe_combine",
    )(pos_flat, ys, gates, x2, g2, final_w)


def _routed_ffn(x2, nw, sc, sh, g2, rw, rb, w1, b1, w2, b2, layer, final_w, seq, final_norm):
    m, d = x2.shape
    rw_pad = jnp.pad(rw, ((0, 0), (0, LANES - N_EXPERTS)))
    rb_pad = jnp.pad(rb, (0, LANES - N_EXPERTS)).reshape(1, LANES)
    h2, eid, gates, rank, cnt = _router(x2, nw, sc, sh, rw_pad, rb_pad, seq)
    tm = EXPERT_TILE
    n_tiles = (m * TOP_K) // tm + N_EXPERTS
    counts = cnt[0, :N_EXPERTS].astype(i32)
    padded = ((counts + tm - 1) // tm) * tm
    gend = jnp.cumsum(padded)
    gstart = gend - padded
    n_used = (gend[-1] // tm).astype(i32).reshape(1)
    tile_start = jnp.arange(n_tiles, dtype=i32) * tm
    tile_expert = jnp.sum((gend[None, :] <= tile_start[:, None]).astype(i32), axis=1)
    tile_expert = jnp.minimum(tile_expert, N_EXPERTS - 1).astype(i32)
    pos = gstart[eid[:, :TOP_K]] + rank[:, :TOP_K]
    pos_flat = pos.reshape(-1).astype(i32)
    xs = _dispatch(h2, pos_flat, n_tiles * tm)
    depth = w1.shape[0]
    ys = _expert_ffn(xs, tile_expert, n_used, w1, b1.reshape(depth, N_EXPERTS, 1, 2 * D_FF),
                     w2, b2.reshape(depth, N_EXPERTS, 1, d), layer)
    return _combine(ys, pos_flat, gates, x2, g2, final_w, seq, final_norm)


def _dsa_prep_kernel(y_ref, qn_ref, wuq_ref, wiq_ref, lnw_ref, lnb_ref, c_ref, s1_ref, s2_ref,
                     q_ref, k_ref, v_ref, qi_ref, ki_ref, wi_ref):
    y = y_ref[...]
    c, s1, s2 = c_ref[...], s1_ref[...], s2_ref[...]
    cq = y[:, :C_Q_RANK]
    cq = (cq * lax.rsqrt(jnp.mean(cq * cq, axis=-1, keepdims=True) + NORM_EPS) * qn_ref[...]).astype(bf16)
    q = jnp.dot(cq, wuq_ref[...], preferred_element_type=f32)
    for j in range(q.shape[1] // LANES):
        sl = slice(j * LANES, (j + 1) * LANES)
        q_ref[sl, :] = (_rot(q[:, sl], c, s1, s2) * (HEAD_DIM ** -0.5)).T.astype(bf16)
    qi = jnp.dot(cq, wiq_ref[...], preferred_element_type=f32)
    for j in range(qi.shape[1] // LANES):
        sl = slice(j * LANES, (j + 1) * LANES)
        qi_ref[:, sl] = (_rot(qi[:, sl], c, s1, s2) * (IDX_DIM ** -0.5)).astype(bf16)
    k_ref[...] = y[:, C_K0:C_V0].astype(bf16)
    ones_col = jnp.where(lax.broadcasted_iota(i32, (y.shape[0], HEAD_DIM), 1) == 0, 1.0, 0.0)
    for g in range(C_KV_HEADS):
        vg = jnp.concatenate([y[:, C_V0 + g * HEAD_DIM:C_V0 + (g + 1) * HEAD_DIM], ones_col], axis=1)
        v_ref[g * LANES:(g + 1) * LANES, :] = vg.T.astype(bf16)
    t6 = y[:, C_IK0:C_IK0 + LANES]
    ik = t6[:, :IDX_DIM]
    mu = jnp.mean(ik, axis=-1, keepdims=True)
    var = jnp.mean(jnp.square(ik - mu), axis=-1, keepdims=True)
    ln = (ik - mu) * lax.rsqrt(var + NORM_EPS) * lnw_ref[...] + lnb_ref[...]
    ln = jnp.concatenate([ln, jnp.zeros_like(ln)], axis=1)
    ki_ref[...] = _rot(ln, c, s1, s2)[:, :IDX_DIM].astype(bf16)
    wi_ref[...] = t6[:, IDX_DIM:IDX_DIM + IDX_HEADS] * (IDX_HEADS ** -0.5)


def _dsa_prep(y, qn, wuq, wiq, lnw, lnb, tabs, tm=256):
    m = y.shape[0]
    row = lambda i: (i, 0)
    fixed = lambda i: (0, 0)
    kvw = C_KV_HEADS * HEAD_DIM
    return pl.pallas_call(
        _dsa_prep_kernel,
        grid=(m // tm,),
        in_specs=[pl.BlockSpec((tm, C_IN_PAD), row),
                  pl.BlockSpec((1, C_Q_RANK), fixed),
                  pl.BlockSpec(wuq.shape, fixed),
                  pl.BlockSpec(wiq.shape, fixed),
                  pl.BlockSpec((1, IDX_DIM), fixed),
                  pl.BlockSpec((1, IDX_DIM), fixed),
                  pl.BlockSpec((tm, LANES), row),
                  pl.BlockSpec((tm, LANES), row),
                  pl.BlockSpec((tm, LANES), row)],
        out_specs=[pl.BlockSpec((C_HEADS * HEAD_DIM, tm), lambda i: (0, i)),
                   pl.BlockSpec((tm, kvw), row),
                   pl.BlockSpec((C_KV_HEADS * LANES, tm), lambda i: (0, i)),
                   pl.BlockSpec((tm, IDX_HEADS * IDX_DIM), row),
                   pl.BlockSpec((tm, IDX_DIM), row),
                   pl.BlockSpec((tm, IDX_HEADS), row)],
        out_shape=[jax.ShapeDtypeStruct((C_HEADS * HEAD_DIM, m), bf16),
                   jax.ShapeDtypeStruct((m, kvw), bf16),
                   jax.ShapeDtypeStruct((C_KV_HEADS * LANES, m), bf16),
                   jax.ShapeDtypeStruct((m, IDX_HEADS * IDX_DIM), bf16),
                   jax.ShapeDtypeStruct((m, IDX_DIM), bf16),
                   jax.ShapeDtypeStruct((m, IDX_HEADS), f32)],
        compiler_params=_params(("parallel",)),
        name="dsa_prep",
    )(y, qn, wuq, wiq, lnw, lnb, *tabs)


def _dsa_select_kernel(qi_ref, wi_ref, ki_ref, mask_ref, key_ref, *, seq, n_sel):
    i = pl.program_id(1)
    qb, kc = C_QBLOCK, KEY_CHUNK
    nkc = (i * qb + qb + kc - 1) // kc
    qs = jnp.concatenate([qi_ref[:, h * IDX_DIM:(h + 1) * IDX_DIM] for h in range(IDX_HEADS)], axis=0)
    wi = wi_ref[...]
    qpos = i * qb + lax.broadcasted_iota(i32, (qb, kc), 0)
    kiota = lax.broadcasted_iota(i32, (qb, kc), 1)

    def score_chunk(c, carry):
        k0 = pl.multiple_of(c * kc, kc)
        lg = lax.dot_general(qs, ki_ref[pl.ds(k0, kc), :], _NT, preferred_element_type=f32)
        sc = wi[:, 0:1] * jnp.maximum(lg[0:qb], 0.0)
        for h in range(1, IDX_HEADS):
            sc = sc + wi[:, h:h + 1] * jnp.maximum(lg[h * qb:(h + 1) * qb], 0.0)
        sc = jnp.where(sc == 0.0, 0.0, sc)
        sc = jnp.where(k0 + kiota <= qpos, sc, -jnp.inf)
        bits = pltpu.bitcast(sc, i32)
        key_ref[:, pl.ds(k0, kc)] = bits ^ ((bits >> 31) & 0x7FFFFFFF)
        return carry

    lax.fori_loop(0, nkc, score_chunk, 0)

    def count(pred):
        def body(c, acc):
            k0 = pl.multiple_of(c * kc, kc)
            hit = jnp.where(pred(key_ref[:, pl.ds(k0, kc)]), 1, 0)
            for j in range(kc // LANES):
                acc = acc + hit[:, j * LANES:(j + 1) * LANES]
            return acc
        acc = lax.fori_loop(0, nkc, body, jnp.zeros((qb, LANES), i32))
        return jnp.sum(acc, axis=-1, keepdims=True)

    thr = jnp.where(count(lambda k: k >= 0) >= n_sel, 0, INT_MIN).astype(i32)

    def bit_step(bi, thr):
        cand = thr | jnp.left_shift(jnp.int32(1), 30 - bi)
        return jnp.where(count(lambda k: k >= cand) >= n_sel, cand, thr)

    thr = lax.fori_loop(0, 31, bit_step, thr)
    allowed = (n_sel - count(lambda k: k > thr)).astype(f32)
    r2 = lax.broadcasted_iota(i32, (kc, kc), 0)
    c2 = lax.broadcasted_iota(i32, (kc, kc), 1)
    upto = jnp.where(r2 <= c2, 1.0, 0.0).astype(bf16)

    def out_chunk(c, ties_before):
        k0 = pl.multiple_of(c * kc, kc)
        key = key_ref[:, pl.ds(k0, kc)]
        eq = jnp.where(key == thr, 1.0, 0.0)
        tie_rank = jnp.dot(eq.astype(bf16), upto, preferred_element_type=f32) + ties_before
        tie_ok = (key == thr) & (tie_rank <= allowed)
        sel = (k0 + kiota <= qpos) & ((key > thr) | tie_ok)
        mask_ref[0, pl.ds(k0, kc), :] = jnp.where(sel, 1.0, 0.0).T.astype(i32).astype(jnp.int8)
        return ties_before + jnp.sum(eq, axis=-1, keepdims=True)

    lax.fori_loop(0, nkc, out_chunk, jnp.zeros((qb, 1), f32))

    def zero_chunk(c, carry):
        k0 = pl.multiple_of(c * kc, kc)
        mask_ref[0, pl.ds(k0, kc), :] = jnp.zeros((kc, qb), jnp.int8)
        return carry

    lax.fori_loop(nkc, seq // kc, zero_chunk, 0)


def _dsa_select(qi, wi, ki, batch, seq):
    nq = seq // C_QBLOCK
    n_sel = min(IDX_TOPK, seq // 4)
    return pl.pallas_call(
        functools.partial(_dsa_select_kernel, seq=seq, n_sel=n_sel),
        grid=(batch, nq),
        in_specs=[pl.BlockSpec((C_QBLOCK, IDX_HEADS * IDX_DIM), lambda b, i: (b * nq + i, 0)),
                  pl.BlockSpec((C_QBLOCK, IDX_HEADS), lambda b, i: (b * nq + i, 0)),
                  pl.BlockSpec((seq, IDX_DIM), lambda b, i: (b, 0))],
        out_specs=pl.BlockSpec((1, seq, C_QBLOCK), lambda b, i: (b, 0, i)),
        out_shape=jax.ShapeDtypeStruct((batch, seq, seq), jnp.int8),
        scratch_shapes=[pltpu.VMEM((C_QBLOCK, seq), i32)],
        compiler_params=_params(("parallel", "parallel")),
        name="dsa_select",
    )(qi, wi, ki)


def _dsa_attn_kernel(qt_ref, k_ref, vt_ref, mask_ref, o_ref, qx_ref, m_ref, acc_ref):
    i = pl.program_id(1)
    qb, kc = C_QBLOCK, ATTN_CHUNK
    g_sz = C_HEADS // C_KV_HEADS
    nkc = (i * qb + qb + kc - 1) // kc
    qx_ref[...] = jnp.zeros_like(qx_ref)
    for g in range(C_KV_HEADS):
        for hl in range(g_sz):
            h = g * g_sz + hl
            qx_ref[g, g * HEAD_DIM:(g + 1) * HEAD_DIM, hl * qb:(hl + 1) * qb] = qt_ref[h * HEAD_DIM:(h + 1) * HEAD_DIM, :]
    m_ref[...] = jnp.full_like(m_ref, NEG_BIG)
    acc_ref[...] = jnp.zeros_like(acc_ref)

    def one_chunk(k0):
        kk = k_ref[pl.ds(k0, kc), :]
        bias = jnp.where(mask_ref[0, pl.ds(k0, kc), :].astype(i32) != 0, 0.0, -jnp.inf)
        bias = jnp.concatenate([bias] * g_sz, axis=1)
        for g in range(C_KV_HEADS):
            s = jnp.dot(kk, qx_ref[g], preferred_element_type=f32) + bias
            m_old = m_ref[g]
            m_new = jnp.maximum(m_old, jnp.max(s, axis=0, keepdims=True))
            p = jnp.exp(s - m_new).astype(bf16)
            acc_ref[g] = (jnp.exp(m_old - m_new) * acc_ref[g]
                          + jnp.dot(vt_ref[g * LANES:(g + 1) * LANES, pl.ds(k0, kc)], p, preferred_element_type=f32))
            m_ref[g] = m_new

    def body(c, carry):
        one_chunk(pl.multiple_of(c * kc, kc))
        return carry

    lax.fori_loop(0, nkc, body, 0)
    outs = []
    for g in range(C_KV_HEADS):
        a = acc_ref[g]
        o = a[:HEAD_DIM] / a[HEAD_DIM:HEAD_DIM + 1]
        for hl in range(g_sz):
            outs.append(o[:, hl * qb:(hl + 1) * qb].T)
    o_ref[...] = jnp.concatenate(outs, axis=1).astype(o_ref.dtype)


def _dsa_attention(qt, k, vt, mask, batch, seq):
    nq = seq // C_QBLOCK
    g_cols = (C_HEADS // C_KV_HEADS) * C_QBLOCK
    kvw = C_KV_HEADS * HEAD_DIM
    return pl.pallas_call(
        _dsa_attn_kernel,
        grid=(batch, nq),
        in_specs=[pl.BlockSpec((C_HEADS * HEAD_DIM, C_QBLOCK), lambda b, i: (0, b * nq + i)),
                  pl.BlockSpec((seq, kvw), lambda b, i: (b, 0)),
                  pl.BlockSpec((C_KV_HEADS * LANES, seq), lambda b, i: (0, b)),
                  pl.BlockSpec((1, seq, C_QBLOCK), lambda b, i: (b, 0, i))],
        out_specs=pl.BlockSpec((C_QBLOCK, C_HEADS * HEAD_DIM), lambda b, i: (b * nq + i, 0)),
        out_shape=jax.ShapeDtypeStruct((batch * seq, C_HEADS * HEAD_DIM), bf16),
        scratch_shapes=[pltpu.VMEM((C_KV_HEADS, kvw, g_cols), bf16),
                        pltpu.VMEM((C_KV_HEADS, 1, g_cols), f32),
                        pltpu.VMEM((C_KV_HEADS, LANES, g_cols), f32)],
        compiler_params=_params(("parallel", "parallel")),
        name="dsa_attention",
    )(qt, k, vt, mask)


def _rotary_tables(positions, rot_dim):
    half = rot_dim // 2
    inv_freq = ROPE_THETA ** (-jnp.arange(0, rot_dim, 2, dtype=f32) / rot_dim)
    ang = positions.astype(f32).reshape(-1, 1) * inv_freq
    cos, sin = jnp.cos(ang), jnp.sin(ang)
    n = ang.shape[0]
    rest = HEAD_DIM - 2 * half
    c = jnp.concatenate([cos, cos, jnp.ones((n, rest), f32)], axis=-1)
    s1 = jnp.concatenate([-sin, jnp.zeros((n, half + rest), f32)], axis=-1)
    s2 = jnp.concatenate([jnp.zeros((n, half), f32), sin, jnp.zeros((n, rest), f32)], axis=-1)
    rep = LANES // HEAD_DIM
    return tuple(jnp.tile(t, (1, rep)) for t in (c, s1, s2))


def kernel(x, c, positions, mod_w, mod_b, norm_mix_w, norm_ffn_w, ab_w_in, ab_w_out, a_sinks, b_lb_logits,
           b_onorm_w, c_w_in, c_q_norm_w, c_w_uq, c_w_iq, c_ik_norm_w, c_ik_norm_b, c_w_out, router_w,
           router_b, moe_w1, moe_b1, moe_w2, moe_b2, final_norm_w):
    batch, seq, d = x.shape
    depth = mod_w.shape[0]
    tabs = _rotary_tables(positions, ROT_DIM)
    tabs_idx = _rotary_tables(positions, IDX_ROT_DIM)
    lower_bounds = jnp.cumsum(jax.nn.softmax(b_lb_logits.astype(f32), axis=0), axis=0)
    cond = jnp.pad(jax.nn.silu(c), ((0, 8 - batch), (0, 0)))
    x2 = x.reshape(batch * seq, d)
    final_w = final_norm_w.reshape(1, d)
    for layer in range(depth):
        mod = _dense(cond, mod_w[layer], mod_b[layer].reshape(1, -1))[:batch]
        sh1, sc1, g1, sh2, sc2, g2 = [t.reshape(batch, 1, d) for t in jnp.split(mod, 6, axis=-1)]
        nw = norm_mix_w[layer].reshape(1, d)
        j = layer // 2
        if layer % 2 == 0:
            proj = _norm_mod_matmul(x2, nw, sc1, sh1, ab_w_in[j].astype(bf16), tabs, seq,
                                    rot_lo=AB_Q0 // LANES, rot_hi=AB_V0 // LANES)
            out_a = _swa_attention(proj, a_sinks[j], batch, seq)
            out_b = _hgrn2(proj, lower_bounds[j].reshape(B_HEADS, 1, B_KEY_DIM),
                           b_onorm_w[j].reshape(1, B_VAL_DIM), batch, seq)
            x2 = _proj_residual([out_a, out_b], ab_w_out[j].astype(bf16), x2, g1, seq)
        else:
            w_in = jnp.pad(c_w_in[j], ((0, 0), (0, C_IN_PAD - c_w_in.shape[-1]))).astype(bf16)
            y = _norm_mod_matmul(x2, nw, sc1, sh1, w_in, tabs, seq,
                                 rot_lo=C_K0 // LANES, rot_hi=C_V0 // LANES)
            q, k, v, qi, ki, wi = _dsa_prep(y, c_q_norm_w[j].reshape(1, -1), c_w_uq[j].astype(bf16),
                                            c_w_iq[j].astype(bf16), c_ik_norm_w[j].reshape(1, -1),
                                            c_ik_norm_b[j].reshape(1, -1), tabs_idx)
            mask = _dsa_select(qi, wi, ki, batch, seq)
            o = _dsa_attention(q, k, v, mask, batch, seq)
            x2 = _proj_residual([o], c_w_out[j].astype(bf16), x2, g1, seq)
        x2 = _routed_ffn(x2, norm_ffn_w[layer].reshape(1, d), sc2, sh2, g2, router_w[layer], router_b[layer],
                         moe_w1, moe_b1, moe_w2, moe_b2, layer, final_w, seq,
                         final_norm=(layer == depth - 1))
    return x2.reshape(batch, seq, d)
```

```python
import functools

import jax
import jax.numpy as jnp
from jax import lax
from jax.experimental import pallas as pl
from jax.experimental.pallas import tpu as pltpu

f32 = jnp.float32
bf16 = jnp.bfloat16
i32 = jnp.int32

D_MODEL = 1024
HEAD_DIM = 64
ROT_DIM = HEAD_DIM // 4
ROPE_THETA = 500000.0
NORM_EPS = 1e-5
A_HEADS = 8
A_KV_HEADS = 2
WINDOW = 128
B_HEADS = 4
B_KEY_DIM = 128
B_VAL_DIM = 128
B_CHUNK = 64
C_HEADS = 16
C_KV_HEADS = 4
C_Q_RANK = 256
IDX_HEADS = 8
IDX_DIM = 64
IDX_ROT_DIM = IDX_DIM // 4
IDX_TOPK = 256
C_QBLOCK = 128
N_EXPERTS = 32
TOP_K = 4
D_FF = D_MODEL
SWIGLU_LIMIT = 7.0
SWIGLU_ALPHA = 1.702

LANES = 128
AB_IN = 2560
AB_Q0, AB_K0, AB_V0 = 0, 512, 640
AB_BQ0, AB_BF0, AB_BI0, AB_BG0 = 768, 1280, 1792, 2304
C_IN_PAD = 896
C_K0, C_V0, C_IK0 = 256, 512, 768

VMEM_LIMIT = 56 * 1024 * 1024
EXPERT_TILE = 256
KEY_CHUNK = 512
ATTN_CHUNK = 512
NEG_BIG = -1e30
INT_MIN = -(2 ** 31)

_NT = (((1,), (1,)), ((), ()))
_TN = (((0,), (0,)), ((), ()))


def _params(sem, vmem=VMEM_LIMIT):
    return pltpu.CompilerParams(dimension_semantics=sem, vmem_limit_bytes=vmem)


def _rot(t, c, s1, s2):
    return t * c + pltpu.roll(t, LANES - ROT_DIM // 2, 1) * s1 + pltpu.roll(t, ROT_DIM // 2, 1) * s2


def _norm_mod(x, nw, sc, sh):
    ms = jnp.mean(x * x, axis=-1, keepdims=True)
    return (x * lax.rsqrt(ms + NORM_EPS) * nw) * (1.0 + sc) + sh


def _dense_kernel(x_ref, w_ref, b_ref, o_ref):
    o_ref[...] = jnp.dot(x_ref[...], w_ref[...], preferred_element_type=f32,
                         precision=lax.Precision.HIGHEST) + b_ref[...]


def _dense(x, w, b, tn=1024):
    m, k = x.shape
    n = w.shape[1]
    return pl.pallas_call(
        _dense_kernel,
        grid=(n // tn,),
        in_specs=[pl.BlockSpec((m, k), lambda j: (0, 0)),
                  pl.BlockSpec((k, tn), lambda j: (0, j)),
                  pl.BlockSpec((1, tn), lambda j: (0, j))],
        out_specs=pl.BlockSpec((m, tn), lambda j: (0, j)),
        out_shape=jax.ShapeDtypeStruct((m, n), f32),
        compiler_params=_params(("parallel",)),
        name="adaln_dense",
    )(x, w, b)


def _nmm_kernel(x_ref, nw_ref, sc_ref, sh_ref, w_ref, c_ref, s1_ref, s2_ref, o_ref, *, rot_lo, rot_hi):
    h = _norm_mod(x_ref[...], nw_ref[...], sc_ref[0], sh_ref[0])
    y = jnp.dot(h.astype(bf16), w_ref[...], preferred_element_type=f32)
    c, s1, s2 = c_ref[...], s1_ref[...], s2_ref[...]
    for j in range(y.shape[1] // LANES):
        yc = y[:, j * LANES:(j + 1) * LANES]
        if rot_lo <= j < rot_hi:
            yc = _rot(yc, c, s1, s2)
        o_ref[:, j * LANES:(j + 1) * LANES] = yc


def _norm_mod_matmul(x2, nw, sc, sh, w, tabs, seq, rot_lo, rot_hi, tm=256):
    m, d = x2.shape
    n = w.shape[1]
    tpb = seq // tm
    row = lambda i: (i, 0)
    per_b = lambda i: (i // tpb, 0, 0)
    return pl.pallas_call(
        functools.partial(_nmm_kernel, rot_lo=rot_lo, rot_hi=rot_hi),
        grid=(m // tm,),
        in_specs=[pl.BlockSpec((tm, d), row),
                  pl.BlockSpec((1, d), lambda i: (0, 0)),
                  pl.BlockSpec((1, 1, d), per_b),
                  pl.BlockSpec((1, 1, d), per_b),
                  pl.BlockSpec((d, n), lambda i: (0, 0)),
                  pl.BlockSpec((tm, LANES), row),
                  pl.BlockSpec((tm, LANES), row),
                  pl.BlockSpec((tm, LANES), row)],
        out_specs=pl.BlockSpec((tm, n), row),
        out_shape=jax.ShapeDtypeStruct((m, n), f32),
        compiler_params=_params(("parallel",)),
        name="norm_mod_proj",
    )(x2, nw, sc, sh, w, *tabs)


def _swa_kernel(sink_ref, q_ref, kp_ref, kc_ref, vp_ref, vc_ref, o_ref):
    i = pl.program_id(1)
    w = WINDOW
    g_sz = A_HEADS // A_KV_HEADS
    q = q_ref[...]
    row = lax.broadcasted_iota(i32, (g_sz * w, 2 * w), 0)
    col = lax.broadcasted_iota(i32, (g_sz * w, 2 * w), 1)
    dist = (row & (w - 1)) + w - col
    lo = jnp.where(i > 0, 0, w)
    valid = (dist >= 0) & (dist < w) & (col >= lo)
    hrow = lax.broadcasted_iota(i32, (g_sz * w, 1), 0) // w
    outs = []
    for g in range(A_KV_HEADS):
        cs = slice(g * HEAD_DIM, (g + 1) * HEAD_DIM)
        kk = jnp.concatenate([kp_ref[:, cs], kc_ref[:, cs]], axis=0).astype(bf16)
        vv = jnp.concatenate([vp_ref[:, cs], vc_ref[:, cs]], axis=0).astype(bf16)
        qg = jnp.concatenate([q[:, (g * g_sz + hl) * HEAD_DIM:(g * g_sz + hl + 1) * HEAD_DIM]
                              for hl in range(g_sz)], axis=0).astype(bf16)
        s = lax.dot_general(qg, kk, _NT, preferred_element_type=f32) * (HEAD_DIM ** -0.5)
        s = jnp.where(valid, s, -jnp.inf)
        sink = jnp.zeros((g_sz * w, 1), f32)
        for hl in range(g_sz):
            sink = jnp.where(hrow == hl, sink_ref[g * g_sz + hl], sink)
        mx = jnp.maximum(jnp.max(s, axis=-1, keepdims=True), sink)
        p = jnp.exp(s - mx)
        p = p / (jnp.sum(p, axis=-1, keepdims=True) + jnp.exp(sink - mx))
        o = jnp.dot(p.astype(bf16), vv, preferred_element_type=f32)
        for hl in range(g_sz):
            outs.append(o[hl * w:(hl + 1) * w])
    o_ref[...] = jnp.concatenate(outs, axis=1).astype(o_ref.dtype)


def _swa_attention(proj, sinks, batch, seq):
    nb = seq // WINDOW
    cur = lambda cb: (lambda b, i: (b * nb + i, cb))
    prev = lambda cb: (lambda b, i: (b * nb + jnp.maximum(i - 1, 0), cb))
    kcb, vcb = AB_K0 // LANES, AB_V0 // LANES
    return pl.pallas_call(
        _swa_kernel,
        grid=(batch, nb),
        in_specs=[pl.BlockSpec(memory_space=pltpu.SMEM),
                  pl.BlockSpec((WINDOW, A_HEADS * HEAD_DIM), cur(0)),
                  pl.BlockSpec((WINDOW, LANES), prev(kcb)),
                  pl.BlockSpec((WINDOW, LANES), cur(kcb)),
                  pl.BlockSpec((WINDOW, LANES), prev(vcb)),
                  pl.BlockSpec((WINDOW, LANES), cur(vcb))],
        out_specs=pl.BlockSpec((WINDOW, A_HEADS * HEAD_DIM), cur(0)),
        out_shape=jax.ShapeDtypeStruct((batch * seq, A_HEADS * HEAD_DIM), bf16),
        compiler_params=_params(("parallel", "parallel")),
        name="swa_attention",
    )(sinks, proj, proj, proj, proj, proj)


def _hgrn_kernel(q_ref, f_ref, i_ref, g_ref, lb_ref, nw_ref, o_ref, st_ref, *, n_sub):
    @pl.when(pl.program_id(2) == 0)
    def _():
        st_ref[...] = jnp.zeros_like(st_ref)

    c = B_CHUNK
    row = lax.broadcasted_iota(i32, (c, B_KEY_DIM), 0)
    r2 = lax.broadcasted_iota(i32, (c, c), 0)
    c2 = lax.broadcasted_iota(i32, (c, c), 1)
    lb = lb_ref[0]
    nw = nw_ref[...]

    def chunk(j, carry):
        r0 = pl.multiple_of(j * c, c)
        rows = pl.ds(r0, c)
        f = lb + (1.0 - lb) * jax.nn.sigmoid(f_ref[rows, :])
        g = jnp.log(f)
        kin = 1.0 - f
        q = q_ref[rows, :]
        qf = q * jax.nn.sigmoid(q) * (B_KEY_DIM ** -0.5)
        iv = i_ref[rows, :].astype(bf16)

        b = g
        d = 1
        while d < c:
            b = b + jnp.where(row >= d, pltpu.roll(b, d, 0), 0.0)
            d *= 2

        scores = jnp.where(r2 == c2, jnp.sum(qf * kin, axis=-1, keepdims=True), 0.0)
        bm = jnp.where(row >= 1, pltpu.roll(b, 1, 0), 0.0)
        m = 1
        while m < c:
            if m > 1:
                bm = jnp.where((row & (m - 1)) < m // 2, bm, pltpu.roll(bm, m // 2, 0))
            right = ((row // m) & 1) == 1
            e = jnp.where(right, b - bm, pltpu.roll(bm, c - m, 0) - b)
            xm = (jnp.where(right, qf, kin) * jnp.exp(e)).astype(bf16)
            y = lax.dot_general(xm, xm, _NT, preferred_element_type=f32)
            pair = (((r2 // m) & 1) == 1) & ((c2 // m) == (r2 // m) - 1)
            scores = scores + jnp.where(pair, y, 0.0)
            m *= 2

        bl = b[c - 1:c, :]
        st = st_ref[...]
        inter = lax.dot_general((qf * jnp.exp(b)).astype(bf16), st.astype(bf16), _NT,
                                preferred_element_type=f32)
        intra = jnp.dot(scores.astype(bf16), iv, preferred_element_type=f32)
        kl = (kin * jnp.exp(bl - b)).astype(bf16)
        st_ref[...] = st * jnp.exp(bl) + lax.dot_general(iv, kl, _TN, preferred_element_type=f32)
        o = inter + intra
        o = o * lax.rsqrt(jnp.mean(o * o, axis=-1, keepdims=True) + NORM_EPS) * nw
        gt = g_ref[rows, :]
        o_ref[rows, :] = (o * (gt * jax.nn.sigmoid(gt))).astype(o_ref.dtype)
        return carry

    lax.fori_loop(0, n_sub, chunk, 0, unroll=4)


def _hgrn2(proj, lb, onorm_w, batch, seq, rows=512):
    n_sub = rows // B_CHUNK
    nr = seq // rows
    blk = lambda c0: pl.BlockSpec((rows, B_KEY_DIM), lambda b, h, r: (b * nr + r, c0 // LANES + h))
    return pl.pallas_call(
        functools.partial(_hgrn_kernel, n_sub=n_sub),
        grid=(batch, B_HEADS, nr),
        in_specs=[blk(AB_BQ0), blk(AB_BF0), blk(AB_BI0), blk(AB_BG0),
                  pl.BlockSpec((1, 1, B_KEY_DIM), lambda b, h, r: (h, 0, 0)),
                  pl.BlockSpec((1, B_VAL_DIM), lambda b, h, r: (0, 0))],
        out_specs=pl.BlockSpec((rows, B_VAL_DIM), lambda b, h, r: (b * nr + r, h)),
        out_shape=jax.ShapeDtypeStruct((batch * seq, B_HEADS * B_VAL_DIM), bf16),
        scratch_shapes=[pltpu.VMEM((B_VAL_DIM, B_KEY_DIM), f32)],
        compiler_params=_params(("parallel", "parallel", "arbitrary")),
        name="hgrn2",
    )(proj, proj, proj, proj, lb, onorm_w)


def _proj_res_kernel(*refs, n_lhs):
    lhs, ws = refs[:n_lhs], refs[n_lhs:2 * n_lhs]
    x_ref, g_ref, o_ref = refs[2 * n_lhs:]
    acc = jnp.dot(lhs[0][...], ws[0][...], preferred_element_type=f32)
    for a, w in zip(lhs[1:], ws[1:]):
        acc = acc + jnp.dot(a[...], w[...], preferred_element_type=f32)
    o_ref[...] = x_ref[...] + g_ref[0] * acc


def _proj_residual(lhs_list, w, x2, gate, seq, tm=512):
    m, d = x2.shape
    kp = lhs_list[0].shape[1]
    n_lhs = len(lhs_list)
    tpb = seq // tm
    row = lambda i: (i, 0)
    in_specs = ([pl.BlockSpec((tm, kp), row) for _ in lhs_list]
                + [pl.BlockSpec((kp, d), (lambda p: (lambda i: (p, 0)))(p)) for p in range(n_lhs)]
                + [pl.BlockSpec((tm, d), row), pl.BlockSpec((1, 1, d), lambda i: (i // tpb, 0, 0))])
    return pl.pallas_call(
        functools.partial(_proj_res_kernel, n_lhs=n_lhs),
        grid=(m // tm,),
        in_specs=in_specs,
        out_specs=pl.BlockSpec((tm, d), row),
        out_shape=jax.ShapeDtypeStruct((m, d), f32),
        compiler_params=_params(("parallel",)),
        name="out_proj_residual",
    )(*lhs_list, *([w] * n_lhs), x2, gate)


def _router_kernel(x_ref, nw_ref, sc_ref, sh_ref, rw_ref, rb_ref,
                   h_ref, eid_ref, gate_ref, rank_ref, cnt_ref, carry_ref):
    @pl.when(pl.program_id(0) == 0)
    def _():
        carry_ref[...] = jnp.zeros_like(carry_ref)

    h = _norm_mod(x_ref[...], nw_ref[...], sc_ref[0], sh_ref[0])
    h_ref[...] = h
    tm = h.shape[0]
    logits = jnp.dot(h, rw_ref[...], preferred_element_type=f32,
                     precision=lax.Precision.HIGHEST) + rb_ref[...]
    lane = lax.broadcasted_iota(i32, (tm, LANES), 1)
    work = jnp.where(lane < N_EXPERTS, logits, -jnp.inf)
    hots, vals, ids = [], [], []
    for _ in range(TOP_K):
        mx = jnp.max(work, axis=-1, keepdims=True)
        idx = jnp.min(jnp.where(work == mx, lane, LANES), axis=-1, keepdims=True)
        hot = lane == idx
        work = jnp.where(hot, -jnp.inf, work)
        hots.append(hot)
        vals.append(mx)
        ids.append(idx)
    es = [jnp.exp(v - vals[0]) for v in vals]
    denom = es[0] + es[1] + es[2] + es[3]
    member = jnp.zeros((tm, LANES), f32)
    for hot in hots:
        member = member + jnp.where(hot, 1.0, 0.0)
    r2 = lax.broadcasted_iota(i32, (tm, tm), 0)
    c2 = lax.broadcasted_iota(i32, (tm, tm), 1)
    before = jnp.where(c2 < r2, 1.0, 0.0).astype(bf16)
    prior = jnp.dot(before, member.astype(bf16), preferred_element_type=f32) + carry_ref[...]
    eid = jnp.zeros((tm, LANES), i32)
    gate = jnp.zeros((tm, LANES), f32)
    rank = jnp.zeros((tm, LANES), i32)
    for k in range(TOP_K):
        rk = jnp.sum(jnp.where(hots[k], prior, 0.0), axis=-1, keepdims=True)
        eid = jnp.where(lane == k, ids[k], eid)
        gate = jnp.where(lane == k, es[k] / denom, gate)
        rank = jnp.where(lane == k, rk.astype(i32), rank)
    eid_ref[...] = eid
    gate_ref[...] = gate
    rank_ref[...] = rank
    carry_ref[...] = carry_ref[...] + jnp.sum(member, axis=0, keepdims=True)
    cnt_ref[...] = carry_ref[...]


def _router(x2, nw, sc, sh, rw, rb, seq, tm=512):
    m, d = x2.shape
    tpb = seq // tm
    row = lambda i: (i, 0)
    per_b = lambda i: (i // tpb, 0, 0)
    fixed = lambda i: (0, 0)
    return pl.pallas_call(
        _router_kernel,
        grid=(m // tm,),
        in_specs=[pl.BlockSpec((tm, d), row),
                  pl.BlockSpec((1, d), fixed),
                  pl.BlockSpec((1, 1, d), per_b),
                  pl.BlockSpec((1, 1, d), per_b),
                  pl.BlockSpec((d, LANES), fixed),
                  pl.BlockSpec((1, LANES), fixed)],
        out_specs=[pl.BlockSpec((tm, d), row),
                   pl.BlockSpec((tm, LANES), row),
                   pl.BlockSpec((tm, LANES), row),
                   pl.BlockSpec((tm, LANES), row),
                   pl.BlockSpec((1, LANES), fixed)],
        out_shape=[jax.ShapeDtypeStruct((m, d), f32),
                   jax.ShapeDtypeStruct((m, LANES), i32),
                   jax.ShapeDtypeStruct((m, LANES), f32),
                   jax.ShapeDtypeStruct((m, LANES), i32),
                   jax.ShapeDtypeStruct((1, LANES), f32)],
        scratch_shapes=[pltpu.VMEM((1, LANES), f32)],
        compiler_params=_params(("arbitrary",)),
        name="ffn_norm_router",
    )(x2, nw, sc, sh, rw, rb)


def _dispatch_kernel(gend_ref, pos_ref, h_ref, out_ref, zero_ref, sem, zsem):
    tm = h_ref.shape[0]

    @pl.when(pl.program_id(0) == 0)
    def _():
        zero_ref[...] = jnp.zeros_like(zero_ref)

        def zcopy(e):
            r0 = pl.multiple_of(jnp.maximum(gend_ref[e] - EXPERT_TILE, 0), EXPERT_TILE)
            return pltpu.make_async_copy(zero_ref, out_ref.at[pl.ds(r0, EXPERT_TILE), :], zsem)

        def zissue(e, carry):
            zcopy(e).start()
            return carry

        def zdrain(e, carry):
            zcopy(e).wait()
            return carry

        lax.fori_loop(0, N_EXPERTS, zissue, 0)
        lax.fori_loop(0, N_EXPERTS, zdrain, 0)

    def copy(r, k):
        return pltpu.make_async_copy(h_ref.at[pl.ds(r, 1), :],
                                     out_ref.at[pl.ds(pos_ref[TOP_K * r + k], 1), :], sem)

    def issue(r, carry):
        for k in range(TOP_K):
            copy(r, k).start()
        return carry

    def drain(r, carry):
        for k in range(TOP_K):
            copy(r, k).wait()
        return carry

    lax.fori_loop(0, tm, issue, 0, unroll=4)
    lax.fori_loop(0, tm, drain, 0, unroll=4)


def _dispatch(h2, pos_flat, gend, n_rows, tm=512):
    m, d = h2.shape
    return pl.pallas_call(
        _dispatch_kernel,
        grid_spec=pltpu.PrefetchScalarGridSpec(
            num_scalar_prefetch=1,
            grid=(m // tm,),
            in_specs=[pl.BlockSpec((tm * TOP_K,), lambda i, ge: (i,), memory_space=pltpu.SMEM),
                      pl.BlockSpec((tm, d), lambda i, ge: (i, 0))],
            out_specs=pl.BlockSpec(memory_space=pl.ANY),
            scratch_shapes=[pltpu.VMEM((EXPERT_TILE, d), f32),
                            pltpu.SemaphoreType.DMA(()), pltpu.SemaphoreType.DMA(())]),
        out_shape=jax.ShapeDtypeStruct((n_rows, d), f32),
        compiler_params=_params(("arbitrary",)),
        name="moe_dispatch",
    )(gend, pos_flat, h2)


def _expert_kernel(te_ref, nu_ref, x_ref, w1_ref, b1_ref, w2_ref, b2_ref, o_ref, w1b_ref, w2b_ref):
    i = pl.program_id(0)
    e = te_ref[i]
    e_prev = te_ref[jnp.maximum(i - 1, 0)]

    @pl.when(jnp.logical_or(i == 0, e != e_prev))
    def _():
        w1b_ref[...] = w1_ref[0, 0].astype(bf16)
        w2b_ref[...] = w2_ref[0, 0].astype(bf16)

    @pl.when(i < nu_ref[0])
    def _():
        x = x_ref[...].astype(bf16)
        gu = jnp.dot(x, w1b_ref[...], preferred_element_type=f32) + b1_ref[0, 0]
        g = jnp.minimum(gu[:, :D_FF], SWIGLU_LIMIT)
        u = jnp.clip(gu[:, D_FF:], -SWIGLU_LIMIT, SWIGLU_LIMIT)
        act = g * jax.nn.sigmoid(SWIGLU_ALPHA * g) * (u + 1.0)
        o_ref[...] = jnp.dot(act.astype(bf16), w2b_ref[...], preferred_element_type=f32) + b2_ref[0, 0]

    @pl.when(i >= nu_ref[0])
    def _():
        o_ref[...] = jnp.zeros_like(o_ref)


def _expert_ffn(xs, tile_expert, n_used, w1, b1, w2, b2, layer):
    n_rows, d = xs.shape
    tm = EXPERT_TILE
    by_e = lambda i, te, nu: (layer, te[i], 0, 0)
    return pl.pallas_call(
        _expert_kernel,
        grid_spec=pltpu.PrefetchScalarGridSpec(
            num_scalar_prefetch=2,
            grid=(n_rows // tm,),
            in_specs=[pl.BlockSpec((tm, d), lambda i, te, nu: (i, 0)),
                      pl.BlockSpec((1, 1, d, 2 * D_FF), by_e),
                      pl.BlockSpec((1, 1, 1, 2 * D_FF), by_e),
                      pl.BlockSpec((1, 1, D_FF, d), by_e),
                      pl.BlockSpec((1, 1, 1, d), by_e)],
            out_specs=pl.BlockSpec((tm, d), lambda i, te, nu: (i, 0)),
            scratch_shapes=[pltpu.VMEM((d, 2 * D_FF), bf16), pltpu.VMEM((D_FF, d), bf16)]),
        out_shape=jax.ShapeDtypeStruct((n_rows, d), f32),
        compiler_params=_params(("arbitrary",)),
        name="expert_ffn",
    )(tile_expert, n_used, xs, w1, b1, w2, b2)


def _combine_kernel(pos_ref, ys_ref, gate_ref, x_ref, g2_ref, fw_ref, o_ref, buf_ref, sem, *, final_norm):
    tm = x_ref.shape[0]

    def copy(r, k):
        return pltpu.make_async_copy(ys_ref.at[pl.ds(pos_ref[TOP_K * r + k], 1), :],
                                     buf_ref.at[k, pl.ds(r, 1), :], sem)

    def issue(r, carry):
        for k in range(TOP_K):
            copy(r, k).start()
        return carry

    def drain(r, carry):
        for k in range(TOP_K):
            copy(r, k).wait()
        return carry

    lax.fori_loop(0, tm, issue, 0, unroll=4)
    lax.fori_loop(0, tm, drain, 0, unroll=4)
    gates = gate_ref[...]
    acc = gates[:, 0:1] * buf_ref[0]
    for k in range(1, TOP_K):
        acc = acc + gates[:, k:k + 1] * buf_ref[k]
    y = x_ref[...] + g2_ref[0] * acc
    if final_norm:
        y = y * lax.rsqrt(jnp.mean(y * y, axis=-1, keepdims=True) + NORM_EPS) * fw_ref[...]
    o_ref[...] = y


def _combine(ys, pos_flat, gates, x2, g2, final_w, seq, final_norm, tm=256):
    m, d = x2.shape
    tpb = seq // tm
    row = lambda i: (i, 0)
    return pl.pallas_call(
        functools.partial(_combine_kernel, final_norm=final_norm),
        grid=(m // tm,),
        in_specs=[pl.BlockSpec((tm * TOP_K,), lambda i: (i,), memory_space=pltpu.SMEM),
                  pl.BlockSpec(memory_space=pl.ANY),
                  pl.BlockSpec((tm, LANES), row),
                  pl.BlockSpec((tm, d), row),
                  pl.BlockSpec((1, 1, d), lambda i: (i // tpb, 0, 0)),
                  pl.BlockSpec((1, d), lambda i: (0, 0))],
        out_specs=pl.BlockSpec((tm, d), row),
        out_shape=jax.ShapeDtypeStruct((m, d), f32),
        scratch_shapes=[pltpu.VMEM((TOP_K, tm, d), f32), pltpu.SemaphoreType.DMA(())],
        compiler_params=_params(("arbitrary",)),
        name="moe_combine",
    )(pos_flat, ys, gates, x2, g2, final_w)


def _routed_ffn(x2, nw, sc, sh, g2, rw, rb, w1, b1, w2, b2, layer, final_w, seq, final_norm):
    m, d = x2.shape
    rw_pad = jnp.pad(rw, ((0, 0), (0, LANES - N_EXPERTS)))
    rb_pad = jnp.pad(rb, (0, LANES - N_EXPERTS)).reshape(1, LANES)
    h2, eid, gates, rank, cnt = _router(x2, nw, sc, sh, rw_pad, rb_pad, seq)
    tm = EXPERT_TILE
    n_tiles = (m * TOP_K) // tm + N_EXPERTS
    counts = cnt[0, :N_EXPERTS].astype(i32)
    padded = ((counts + tm - 1) // tm) * tm
    gend = jnp.cumsum(padded)
    gstart = gend - padded
    n_used = (gend[-1] // tm).astype(i32).reshape(1)
    tile_start = jnp.arange(n_tiles, dtype=i32) * tm
    tile_expert = jnp.sum((gend[None, :] <= tile_start[:, None]).astype(i32), axis=1)
    tile_expert = jnp.minimum(tile_expert, N_EXPERTS - 1).astype(i32)
    pos = gstart[eid[:, :TOP_K]] + rank[:, :TOP_K]
    pos_flat = pos.reshape(-1).astype(i32)
    xs = _dispatch(h2, pos_flat, gend.astype(i32), n_tiles * tm)
    depth = w1.shape[0]
    ys = _expert_ffn(xs, tile_expert, n_used, w1, b1.reshape(depth, N_EXPERTS, 1, 2 * D_FF),
                     w2, b2.reshape(depth, N_EXPERTS, 1, d), layer)
    return _combine(ys, pos_flat, gates, x2, g2, final_w, seq, final_norm)


def _dsa_prep_kernel(y_ref, qn_ref, wuq_ref, wiq_ref, lnw_ref, lnb_ref, c_ref, s1_ref, s2_ref,
                     q_ref, k_ref, v_ref, qi_ref, ki_ref, wi_ref):
    y = y_ref[...]
    c, s1, s2 = c_ref[...], s1_ref[...], s2_ref[...]
    cq = y[:, :C_Q_RANK]
    cq = (cq * lax.rsqrt(jnp.mean(cq * cq, axis=-1, keepdims=True) + NORM_EPS) * qn_ref[...]).astype(bf16)
    q = jnp.dot(cq, wuq_ref[...], preferred_element_type=f32)
    for j in range(q.shape[1] // LANES):
        sl = slice(j * LANES, (j + 1) * LANES)
        q_ref[sl, :] = (_rot(q[:, sl], c, s1, s2) * (HEAD_DIM ** -0.5)).T.astype(bf16)
    qi = jnp.dot(cq, wiq_ref[...], preferred_element_type=f32)
    for j in range(qi.shape[1] // LANES):
        sl = slice(j * LANES, (j + 1) * LANES)
        qi_ref[:, sl] = (_rot(qi[:, sl], c, s1, s2) * (IDX_DIM ** -0.5)).astype(bf16)
    k_ref[...] = y[:, C_K0:C_V0].astype(bf16)
    ones_col = jnp.where(lax.broadcasted_iota(i32, (y.shape[0], HEAD_DIM), 1) == 0, 1.0, 0.0)
    for g in range(C_KV_HEADS):
        vg = jnp.concatenate([y[:, C_V0 + g * HEAD_DIM:C_V0 + (g + 1) * HEAD_DIM], ones_col], axis=1)
        v_ref[g * LANES:(g + 1) * LANES, :] = vg.T.astype(bf16)
    t6 = y[:, C_IK0:C_IK0 + LANES]
    ik = t6[:, :IDX_DIM]
    mu = jnp.mean(ik, axis=-1, keepdims=True)
    var = jnp.mean(jnp.square(ik - mu), axis=-1, keepdims=True)
    ln = (ik - mu) * lax.rsqrt(var + NORM_EPS) * lnw_ref[...] + lnb_ref[...]
    ln = jnp.concatenate([ln, jnp.zeros_like(ln)], axis=1)
    ki_ref[...] = _rot(ln, c, s1, s2)[:, :IDX_DIM].astype(bf16)
    wi_ref[...] = t6[:, IDX_DIM:IDX_DIM + IDX_HEADS] * (IDX_HEADS ** -0.5)


def _dsa_prep(y, qn, wuq, wiq, lnw, lnb, tabs, tm=256):
    m = y.shape[0]
    row = lambda i: (i, 0)
    fixed = lambda i: (0, 0)
    kvw = C_KV_HEADS * HEAD_DIM
    return pl.pallas_call(
        _dsa_prep_kernel,
        grid=(m // tm,),
        in_specs=[pl.BlockSpec((tm, C_IN_PAD), row),
                  pl.BlockSpec((1, C_Q_RANK), fixed),
                  pl.BlockSpec(wuq.shape, fixed),
                  pl.BlockSpec(wiq.shape, fixed),
                  pl.BlockSpec((1, IDX_DIM), fixed),
                  pl.BlockSpec((1, IDX_DIM), fixed),
                  pl.BlockSpec((tm, LANES), row),
                  pl.BlockSpec((tm, LANES), row),
                  pl.BlockSpec((tm, LANES), row)],
        out_specs=[pl.BlockSpec((C_HEADS * HEAD_DIM, tm), lambda i: (0, i)),
                   pl.BlockSpec((tm, kvw), row),
                   pl.BlockSpec((C_KV_HEADS * LANES, tm), lambda i: (0, i)),
                   pl.BlockSpec((tm, IDX_HEADS * IDX_DIM), row),
                   pl.BlockSpec((tm, IDX_DIM), row),
                   pl.BlockSpec((tm, IDX_HEADS), row)],
        out_shape=[jax.ShapeDtypeStruct((C_HEADS * HEAD_DIM, m), bf16),
                   jax.ShapeDtypeStruct((m, kvw), bf16),
                   jax.ShapeDtypeStruct((C_KV_HEADS * LANES, m), bf16),
                   jax.ShapeDtypeStruct((m, IDX_HEADS * IDX_DIM), bf16),
                   jax.ShapeDtypeStruct((m, IDX_DIM), bf16),
                   jax.ShapeDtypeStruct((m, IDX_HEADS), f32)],
        compiler_params=_params(("parallel",)),
        name="dsa_prep",
    )(y, qn, wuq, wiq, lnw, lnb, *tabs)


def _dsa_select_kernel(qi_ref, wi_ref, ki_ref, mask_ref, key_ref, *, seq, n_sel):
    i = pl.program_id(1)
    qb, kc = C_QBLOCK, KEY_CHUNK
    nkc = (i * qb + qb + kc - 1) // kc
    qs = jnp.concatenate([qi_ref[:, h * IDX_DIM:(h + 1) * IDX_DIM] for h in range(IDX_HEADS)], axis=0)
    wi = wi_ref[...]
    qpos = i * qb + lax.broadcasted_iota(i32, (qb, kc), 0)
    kiota = lax.broadcasted_iota(i32, (qb, kc), 1)

    def score_chunk(c, carry):
        k0 = pl.multiple_of(c * kc, kc)
        lg = lax.dot_general(qs, ki_ref[pl.ds(k0, kc), :], _NT, preferred_element_type=f32)
        sc = wi[:, 0:1] * jnp.maximum(lg[0:qb], 0.0)
        for h in range(1, IDX_HEADS):
            sc = sc + wi[:, h:h + 1] * jnp.maximum(lg[h * qb:(h + 1) * qb], 0.0)
        sc = jnp.where(sc == 0.0, 0.0, sc)
        sc = jnp.where(k0 + kiota <= qpos, sc, -jnp.inf)
        bits = pltpu.bitcast(sc, i32)
        key_ref[:, pl.ds(k0, kc)] = bits ^ ((bits >> 31) & 0x7FFFFFFF)
        return carry

    lax.fori_loop(0, nkc, score_chunk, 0)

    def count(pred):
        def body(c, acc):
            k0 = pl.multiple_of(c * kc, kc)
            hit = jnp.where(pred(key_ref[:, pl.ds(k0, kc)]), 1, 0)
            for j in range(kc // LANES):
                acc = acc + hit[:, j * LANES:(j + 1) * LANES]
            return acc
        acc = lax.fori_loop(0, nkc, body, jnp.zeros((qb, LANES), i32))
        return jnp.sum(acc, axis=-1, keepdims=True)

    thr = jnp.where(count(lambda k: k >= 0) >= n_sel, 0, INT_MIN).astype(i32)

    def bit_step(bi, thr):
        cand = thr | jnp.left_shift(jnp.int32(1), 30 - bi)
        return jnp.where(count(lambda k: k >= cand) >= n_sel, cand, thr)

    thr = lax.fori_loop(0, 31, bit_step, thr)
    allowed = (n_sel - count(lambda k: k > thr)).astype(f32)
    r2 = lax.broadcasted_iota(i32, (kc, kc), 0)
    c2 = lax.broadcasted_iota(i32, (kc, kc), 1)
    upto = jnp.where(r2 <= c2, 1.0, 0.0).astype(bf16)

    def out_chunk(c, ties_before):
        k0 = pl.multiple_of(c * kc, kc)
        key = key_ref[:, pl.ds(k0, kc)]
        eq = jnp.where(key == thr, 1.0, 0.0)
        tie_rank = jnp.dot(eq.astype(bf16), upto, preferred_element_type=f32) + ties_before
        tie_ok = (key == thr) & (tie_rank <= allowed)
        sel = (k0 + kiota <= qpos) & ((key > thr) | tie_ok)
        mask_ref[0, pl.ds(k0, kc), :] = jnp.where(sel, 1.0, 0.0).T.astype(i32).astype(jnp.int8)
        return ties_before + jnp.sum(eq, axis=-1, keepdims=True)

    lax.fori_loop(0, nkc, out_chunk, jnp.zeros((qb, 1), f32))

    def zero_chunk(c, carry):
        k0 = pl.multiple_of(c * kc, kc)
        mask_ref[0, pl.ds(k0, kc), :] = jnp.zeros((kc, qb), jnp.int8)
        return carry

    lax.fori_loop(nkc, seq // kc, zero_chunk, 0)


def _dsa_select(qi, wi, ki, batch, seq):
    nq = seq // C_QBLOCK
    n_sel = min(IDX_TOPK, seq // 4)
    return pl.pallas_call(
        functools.partial(_dsa_select_kernel, seq=seq, n_sel=n_sel),
        grid=(batch, nq),
        in_specs=[pl.BlockSpec((C_QBLOCK, IDX_HEADS * IDX_DIM), lambda b, i: (b * nq + i, 0)),
                  pl.BlockSpec((C_QBLOCK, IDX_HEADS), lambda b, i: (b * nq + i, 0)),
                  pl.BlockSpec((seq, IDX_DIM), lambda b, i: (b, 0))],
        out_specs=pl.BlockSpec((1, seq, C_QBLOCK), lambda b, i: (b, 0, i)),
        out_shape=jax.ShapeDtypeStruct((batch, seq, seq), jnp.int8),
        scratch_shapes=[pltpu.VMEM((C_QBLOCK, seq), i32)],
        compiler_params=_params(("parallel", "parallel")),
        name="dsa_select",
    )(qi, wi, ki)


def _dsa_attn_kernel(qt_ref, k_ref, vt_ref, mask_ref, o_ref, qx_ref, m_ref, acc_ref):
    i = pl.program_id(1)
    qb, kc = C_QBLOCK, ATTN_CHUNK
    g_sz = C_HEADS // C_KV_HEADS
    nkc = (i * qb + qb + kc - 1) // kc
    qx_ref[...] = jnp.zeros_like(qx_ref)
    for g in range(C_KV_HEADS):
        for hl in range(g_sz):
            h = g * g_sz + hl
            qx_ref[g, g * HEAD_DIM:(g + 1) * HEAD_DIM, hl * qb:(hl + 1) * qb] = qt_ref[h * HEAD_DIM:(h + 1) * HEAD_DIM, :]
    m_ref[...] = jnp.full_like(m_ref, NEG_BIG)
    acc_ref[...] = jnp.zeros_like(acc_ref)

    def one_chunk(k0):
        kk = k_ref[pl.ds(k0, kc), :]
        bias = jnp.where(mask_ref[0, pl.ds(k0, kc), :].astype(i32) != 0, 0.0, -jnp.inf)
        bias = jnp.concatenate([bias] * g_sz, axis=1)
        for g in range(C_KV_HEADS):
            s = jnp.dot(kk, qx_ref[g], preferred_element_type=f32) + bias
            m_old = m_ref[g]
            m_new = jnp.maximum(m_old, jnp.max(s, axis=0, keepdims=True))
            p = jnp.exp(s - m_new).astype(bf16)
            acc_ref[g] = (jnp.exp(m_old - m_new) * acc_ref[g]
                          + jnp.dot(vt_ref[g * LANES:(g + 1) * LANES, pl.ds(k0, kc)], p, preferred_element_type=f32))
            m_ref[g] = m_new

    def body(c, carry):
        one_chunk(pl.multiple_of(c * kc, kc))
        return carry

    lax.fori_loop(0, nkc, body, 0)
    outs = []
    for g in range(C_KV_HEADS):
        a = acc_ref[g]
        o = a[:HEAD_DIM] / a[HEAD_DIM:HEAD_DIM + 1]
        for hl in range(g_sz):
            outs.append(o[:, hl * qb:(hl + 1) * qb].T)
    o_ref[...] = jnp.concatenate(outs, axis=1).astype(o_ref.dtype)


def _dsa_attention(qt, k, vt, mask, batch, seq):
    nq = seq // C_QBLOCK
    g_cols = (C_HEADS // C_KV_HEADS) * C_QBLOCK
    kvw = C_KV_HEADS * HEAD_DIM
    return pl.pallas_call(
        _dsa_attn_kernel,
        grid=(batch, nq),
        in_specs=[pl.BlockSpec((C_HEADS * HEAD_DIM, C_QBLOCK), lambda b, i: (0, b * nq + i)),
                  pl.BlockSpec((seq, kvw), lambda b, i: (b, 0)),
                  pl.BlockSpec((C_KV_HEADS * LANES, seq), lambda b, i: (0, b)),
                  pl.BlockSpec((1, seq, C_QBLOCK), lambda b, i: (b, 0, i))],
        out_specs=pl.BlockSpec((C_QBLOCK, C_HEADS * HEAD_DIM), lambda b, i: (b * nq + i, 0)),
        out_shape=jax.ShapeDtypeStruct((batch * seq, C_HEADS * HEAD_DIM), bf16),
        scratch_shapes=[pltpu.VMEM((C_KV_HEADS, kvw, g_cols), bf16),
                        pltpu.VMEM((C_KV_HEADS, 1, g_cols), f32),
                        pltpu.VMEM((C_KV_HEADS, LANES, g_cols), f32)],
        compiler_params=_params(("parallel", "parallel")),
        name="dsa_attention",
    )(qt, k, vt, mask)


def _rotary_tables(positions, rot_dim):
    half = rot_dim // 2
    inv_freq = ROPE_THETA ** (-jnp.arange(0, rot_dim, 2, dtype=f32) / rot_dim)
    ang = positions.astype(f32).reshape(-1, 1) * inv_freq
    cos, sin = jnp.cos(ang), jnp.sin(ang)
    n = ang.shape[0]
    rest = HEAD_DIM - 2 * half
    c = jnp.concatenate([cos, cos, jnp.ones((n, rest), f32)], axis=-1)
    s1 = jnp.concatenate([-sin, jnp.zeros((n, half + rest), f32)], axis=-1)
    s2 = jnp.concatenate([jnp.zeros((n, half), f32), sin, jnp.zeros((n, rest), f32)], axis=-1)
    rep = LANES // HEAD_DIM
    return tuple(jnp.tile(t, (1, rep)) for t in (c, s1, s2))


def kernel(x, c, positions, mod_w, mod_b, norm_mix_w, norm_ffn_w, ab_w_in, ab_w_out, a_sinks, b_lb_logits,
           b_onorm_w, c_w_in, c_q_norm_w, c_w_uq, c_w_iq, c_ik_norm_w, c_ik_norm_b, c_w_out, router_w,
           router_b, moe_w1, moe_b1, moe_w2, moe_b2, final_norm_w):
    batch, seq, d = x.shape
    depth = mod_w.shape[0]
    tabs = _rotary_tables(positions, ROT_DIM)
    tabs_idx = _rotary_tables(positions, IDX_ROT_DIM)
    lower_bounds = jnp.cumsum(jax.nn.softmax(b_lb_logits.astype(f32), axis=0), axis=0)
    cond = jnp.pad(jax.nn.silu(c), ((0, 8 - batch), (0, 0)))
    x2 = x.reshape(batch * seq, d)
    final_w = final_norm_w.reshape(1, d)
    for layer in range(depth):
        mod = _dense(cond, mod_w[layer], mod_b[layer].reshape(1, -1))[:batch]
        sh1, sc1, g1, sh2, sc2, g2 = [t.reshape(batch, 1, d) for t in jnp.split(mod, 6, axis=-1)]
        nw = norm_mix_w[layer].reshape(1, d)
        j = layer // 2
        if layer % 2 == 0:
            proj = _norm_mod_matmul(x2, nw, sc1, sh1, ab_w_in[j].astype(bf16), tabs, seq,
                                    rot_lo=AB_Q0 // LANES, rot_hi=AB_V0 // LANES)
            out_a = _swa_attention(proj, a_sinks[j], batch, seq)
            out_b = _hgrn2(proj, lower_bounds[j].reshape(B_HEADS, 1, B_KEY_DIM),
                           b_onorm_w[j].reshape(1, B_VAL_DIM), batch, seq)
            x2 = _proj_residual([out_a, out_b], ab_w_out[j].astype(bf16), x2, g1, seq)
        else:
            w_in = jnp.pad(c_w_in[j], ((0, 0), (0, C_IN_PAD - c_w_in.shape[-1]))).astype(bf16)
            y = _norm_mod_matmul(x2, nw, sc1, sh1, w_in, tabs, seq,
                                 rot_lo=C_K0 // LANES, rot_hi=C_V0 // LANES)
            q, k, v, qi, ki, wi = _dsa_prep(y, c_q_norm_w[j].reshape(1, -1), c_w_uq[j].astype(bf16),
                                            c_w_iq[j].astype(bf16), c_ik_norm_w[j].reshape(1, -1),
                                            c_ik_norm_b[j].reshape(1, -1), tabs_idx)
            mask = _dsa_select(qi, wi, ki, batch, seq)
            o = _dsa_attention(q, k, v, mask, batch, seq)
            x2 = _proj_residual([o], c_w_out[j].astype(bf16), x2, g1, seq)
        x2 = _routed_ffn(x2, norm_ffn_w[layer].reshape(1, d), sc2, sh2, g2, router_w[layer], router_b[layer],
                         moe_w1, moe_b1, moe_w2, moe_b2, layer, final_w, seq,
                         final_norm=(layer == depth - 1))
    return x2.reshape(batch, seq, d)
```

```python
import functools

import jax
import jax.numpy as jnp
from jax import lax
from jax.experimental import pallas as pl
from jax.experimental.pallas import tpu as pltpu

f32 = jnp.float32
bf16 = jnp.bfloat16
i32 = jnp.int32

D_MODEL = 1024
HEAD_DIM = 64
ROT_DIM = HEAD_DIM // 4
ROPE_THETA = 500000.0
NORM_EPS = 1e-5
A_HEADS = 8
A_KV_HEADS = 2
WINDOW = 128
B_HEADS = 4
B_KEY_DIM = 128
B_VAL_DIM = 128
B_CHUNK = 64
C_HEADS = 16
C_KV_HEADS = 4
C_Q_RANK = 256
IDX_HEADS = 8
IDX_DIM = 64
IDX_ROT_DIM = IDX_DIM // 4
IDX_TOPK = 256
C_QBLOCK = 128
N_EXPERTS = 32
TOP_K = 4
D_FF = D_MODEL
SWIGLU_LIMIT = 7.0
SWIGLU_ALPHA = 1.702

LANES = 128
AB_IN = 2560
AB_Q0, AB_K0, AB_V0 = 0, 512, 640
AB_BQ0, AB_BF0, AB_BI0, AB_BG0 = 768, 1280, 1792, 2304
C_IN_PAD = 896
C_K0, C_V0, C_IK0 = 256, 512, 768

VMEM_LIMIT = 56 * 1024 * 1024
EXPERT_TILE = 256
KEY_CHUNK = 512
ATTN_CHUNK = 512
NEG_BIG = -1e30
INT_MIN = -(2 ** 31)

_NT = (((1,), (1,)), ((), ()))
_TN = (((0,), (0,)), ((), ()))


def _params(sem, vmem=VMEM_LIMIT):
    return pltpu.CompilerParams(dimension_semantics=sem, vmem_limit_bytes=vmem)


def _rot(t, c, s1, s2):
    return t * c + pltpu.roll(t, LANES - ROT_DIM // 2, 1) * s1 + pltpu.roll(t, ROT_DIM // 2, 1) * s2


def _norm_mod(x, nw, sc, sh):
    ms = jnp.mean(x * x, axis=-1, keepdims=True)
    return (x * lax.rsqrt(ms + NORM_EPS) * nw) * (1.0 + sc) + sh


def _dense_kernel(x_ref, w_ref, b_ref, o_ref):
    o_ref[...] = jnp.dot(x_ref[...], w_ref[...], preferred_element_type=f32,
                         precision=lax.Precision.HIGHEST) + b_ref[...]


def _dense(x, w, b, tn=1024):
    m, k = x.shape
    n = w.shape[1]
    return pl.pallas_call(
        _dense_kernel,
        grid=(n // tn,),
        in_specs=[pl.BlockSpec((m, k), lambda j: (0, 0)),
                  pl.BlockSpec((k, tn), lambda j: (0, j)),
                  pl.BlockSpec((1, tn), lambda j: (0, j))],
        out_specs=pl.BlockSpec((m, tn), lambda j: (0, j)),
        out_shape=jax.ShapeDtypeStruct((m, n), f32),
        compiler_params=_params(("parallel",)),
        name="adaln_dense",
    )(x, w, b)


def _nmm_kernel(x_ref, nw_ref, sc_ref, sh_ref, w_ref, c_ref, s1_ref, s2_ref, o_ref, *, rot_lo, rot_hi):
    h = _norm_mod(x_ref[...], nw_ref[...], sc_ref[0], sh_ref[0])
    y = jnp.dot(h.astype(bf16), w_ref[...], preferred_element_type=f32)
    c, s1, s2 = c_ref[...], s1_ref[...], s2_ref[...]
    for j in range(y.shape[1] // LANES):
        yc = y[:, j * LANES:(j + 1) * LANES]
        if rot_lo <= j < rot_hi:
            yc = _rot(yc, c, s1, s2)
        o_ref[:, j * LANES:(j + 1) * LANES] = yc


def _norm_mod_matmul(x2, nw, sc, sh, w, tabs, seq, rot_lo, rot_hi, tm=256):
    m, d = x2.shape
    n = w.shape[1]
    tpb = seq // tm
    row = lambda i: (i, 0)
    per_b = lambda i: (i // tpb, 0, 0)
    return pl.pallas_call(
        functools.partial(_nmm_kernel, rot_lo=rot_lo, rot_hi=rot_hi),
        grid=(m // tm,),
        in_specs=[pl.BlockSpec((tm, d), row),
                  pl.BlockSpec((1, d), lambda i: (0, 0)),
                  pl.BlockSpec((1, 1, d), per_b),
                  pl.BlockSpec((1, 1, d), per_b),
                  pl.BlockSpec((d, n), lambda i: (0, 0)),
                  pl.BlockSpec((tm, LANES), row),
                  pl.BlockSpec((tm, LANES), row),
                  pl.BlockSpec((tm, LANES), row)],
        out_specs=pl.BlockSpec((tm, n), row),
        out_shape=jax.ShapeDtypeStruct((m, n), f32),
        compiler_params=_params(("parallel",)),
        name="norm_mod_proj",
    )(x2, nw, sc, sh, w, *tabs)


def _swa_kernel(sink_ref, q_ref, kp_ref, kc_ref, vp_ref, vc_ref, o_ref):
    i = pl.program_id(1)
    w = WINDOW
    g_sz = A_HEADS // A_KV_HEADS
    q = q_ref[...]
    row = lax.broadcasted_iota(i32, (g_sz * w, 2 * w), 0)
    col = lax.broadcasted_iota(i32, (g_sz * w, 2 * w), 1)
    dist = (row & (w - 1)) + w - col
    lo = jnp.where(i > 0, 0, w)
    valid = (dist >= 0) & (dist < w) & (col >= lo)
    hrow = lax.broadcasted_iota(i32, (g_sz * w, 1), 0) // w
    outs = []
    for g in range(A_KV_HEADS):
        cs = slice(g * HEAD_DIM, (g + 1) * HEAD_DIM)
        kk = jnp.concatenate([kp_ref[:, cs], kc_ref[:, cs]], axis=0).astype(bf16)
        vv = jnp.concatenate([vp_ref[:, cs], vc_ref[:, cs]], axis=0).astype(bf16)
        qg = jnp.concatenate([q[:, (g * g_sz + hl) * HEAD_DIM:(g * g_sz + hl + 1) * HEAD_DIM]
                              for hl in range(g_sz)], axis=0).astype(bf16)
        s = lax.dot_general(qg, kk, _NT, preferred_element_type=f32) * (HEAD_DIM ** -0.5)
        s = jnp.where(valid, s, -jnp.inf)
        sink = jnp.zeros((g_sz * w, 1), f32)
        for hl in range(g_sz):
            sink = jnp.where(hrow == hl, sink_ref[g * g_sz + hl], sink)
        mx = jnp.maximum(jnp.max(s, axis=-1, keepdims=True), sink)
        p = jnp.exp(s - mx)
        p = p / (jnp.sum(p, axis=-1, keepdims=True) + jnp.exp(sink - mx))
        o = jnp.dot(p.astype(bf16), vv, preferred_element_type=f32)
        for hl in range(g_sz):
            outs.append(o[hl * w:(hl + 1) * w])
    o_ref[...] = jnp.concatenate(outs, axis=1).astype(o_ref.dtype)


def _swa_attention(proj, sinks, batch, seq):
    nb = seq // WINDOW
    cur = lambda cb: (lambda b, i: (b * nb + i, cb))
    prev = lambda cb: (lambda b, i: (b * nb + jnp.maximum(i - 1, 0), cb))
    kcb, vcb = AB_K0 // LANES, AB_V0 // LANES
    return pl.pallas_call(
        _swa_kernel,
        grid=(batch, nb),
        in_specs=[pl.BlockSpec(memory_space=pltpu.SMEM),
                  pl.BlockSpec((WINDOW, A_HEADS * HEAD_DIM), cur(0)),
                  pl.BlockSpec((WINDOW, LANES), prev(kcb)),
                  pl.BlockSpec((WINDOW, LANES), cur(kcb)),
                  pl.BlockSpec((WINDOW, LANES), prev(vcb)),
                  pl.BlockSpec((WINDOW, LANES), cur(vcb))],
        out_specs=pl.BlockSpec((WINDOW, A_HEADS * HEAD_DIM), cur(0)),
        out_shape=jax.ShapeDtypeStruct((batch * seq, A_HEADS * HEAD_DIM), bf16),
        compiler_params=_params(("parallel", "parallel")),
        name="swa_attention",
    )(sinks, proj, proj, proj, proj, proj)


def _hgrn_kernel(q_ref, f_ref, i_ref, g_ref, lb_ref, nw_ref, o_ref, st_ref, *, n_sub):
    @pl.when(pl.program_id(2) == 0)
    def _():
        st_ref[...] = jnp.zeros_like(st_ref)

    c = B_CHUNK
    row = lax.broadcasted_iota(i32, (c, B_KEY_DIM), 0)
    r2 = lax.broadcasted_iota(i32, (c, c), 0)
    c2 = lax.broadcasted_iota(i32, (c, c), 1)
    lb = lb_ref[0]
    nw = nw_ref[...]

    def chunk(j, carry):
        r0 = pl.multiple_of(j * c, c)
        rows = pl.ds(r0, c)
        f = lb + (1.0 - lb) * jax.nn.sigmoid(f_ref[rows, :])
        g = jnp.log(f)
        kin = 1.0 - f
        q = q_ref[rows, :]
        qf = q * jax.nn.sigmoid(q) * (B_KEY_DIM ** -0.5)
        iv = i_ref[rows, :].astype(bf16)

        b = g
        d = 1
        while d < c:
            b = b + jnp.where(row >= d, pltpu.roll(b, d, 0), 0.0)
            d *= 2

        scores = jnp.where(r2 == c2, jnp.sum(qf * kin, axis=-1, keepdims=True), 0.0)
        bm = jnp.where(row >= 1, pltpu.roll(b, 1, 0), 0.0)
        m = 1
        while m < c:
            if m > 1:
                bm = jnp.where((row & (m - 1)) < m // 2, bm, pltpu.roll(bm, m // 2, 0))
            right = ((row // m) & 1) == 1
            e = jnp.where(right, b - bm, pltpu.roll(bm, c - m, 0) - b)
            xm = (jnp.where(right, qf, kin) * jnp.exp(e)).astype(bf16)
            y = lax.dot_general(xm, xm, _NT, preferred_element_type=f32)
            pair = (((r2 // m) & 1) == 1) & ((c2 // m) == (r2 // m) - 1)
            scores = scores + jnp.where(pair, y, 0.0)
            m *= 2

        bl = b[c - 1:c, :]
        st = st_ref[...]
        inter = lax.dot_general((qf * jnp.exp(b)).astype(bf16), st.astype(bf16), _NT,
                                preferred_element_type=f32)
        intra = jnp.dot(scores.astype(bf16), iv, preferred_element_type=f32)
        kl = (kin * jnp.exp(bl - b)).astype(bf16)
        st_ref[...] = st * jnp.exp(bl) + lax.dot_general(iv, kl, _TN, preferred_element_type=f32)
        o = inter + intra
        o = o * lax.rsqrt(jnp.mean(o * o, axis=-1, keepdims=True) + NORM_EPS) * nw
        gt = g_ref[rows, :]
        o_ref[rows, :] = (o * (gt * jax.nn.sigmoid(gt))).astype(o_ref.dtype)
        return carry

    lax.fori_loop(0, n_sub, chunk, 0, unroll=4)


def _hgrn2(proj, lb, onorm_w, batch, seq, rows=512):
    n_sub = rows // B_CHUNK
    nr = seq // rows
    blk = lambda c0: pl.BlockSpec((rows, B_KEY_DIM), lambda b, h, r: (b * nr + r, c0 // LANES + h))
    return pl.pallas_call(
        functools.partial(_hgrn_kernel, n_sub=n_sub),
        grid=(batch, B_HEADS, nr),
        in_specs=[blk(AB_BQ0), blk(AB_BF0), blk(AB_BI0), blk(AB_BG0),
                  pl.BlockSpec((1, 1, B_KEY_DIM), lambda b, h, r: (h, 0, 0)),
                  pl.BlockSpec((1, B_VAL_DIM), lambda b, h, r: (0, 0))],
        out_specs=pl.BlockSpec((rows, B_VAL_DIM), lambda b, h, r: (b * nr + r, h)),
        out_shape=jax.ShapeDtypeStruct((batch * seq, B_HEADS * B_VAL_DIM), bf16),
        scratch_shapes=[pltpu.VMEM((B_VAL_DIM, B_KEY_DIM), f32)],
        compiler_params=_params(("parallel", "parallel", "arbitrary")),
        name="hgrn2",
    )(proj, proj, proj, proj, lb, onorm_w)


def _proj_res_kernel(*refs, n_lhs):
    lhs, ws = refs[:n_lhs], refs[n_lhs:2 * n_lhs]
    x_ref, g_ref, o_ref = refs[2 * n_lhs:]
    acc = jnp.dot(lhs[0][...], ws[0][...], preferred_element_type=f32)
    for a, w in zip(lhs[1:], ws[1:]):
        acc = acc + jnp.dot(a[...], w[...], preferred_element_type=f32)
    o_ref[...] = x_ref[...] + g_ref[0] * acc


def _proj_residual(lhs_list, w, x2, gate, seq, tm=512):
    m, d = x2.shape
    kp = lhs_list[0].shape[1]
    n_lhs = len(lhs_list)
    tpb = seq // tm
    row = lambda i: (i, 0)
    in_specs = ([pl.BlockSpec((tm, kp), row) for _ in lhs_list]
                + [pl.BlockSpec((kp, d), (lambda p: (lambda i: (p, 0)))(p)) for p in range(n_lhs)]
                + [pl.BlockSpec((tm, d), row), pl.BlockSpec((1, 1, d), lambda i: (i // tpb, 0, 0))])
    return pl.pallas_call(
        functools.partial(_proj_res_kernel, n_lhs=n_lhs),
        grid=(m // tm,),
        in_specs=in_specs,
        out_specs=pl.BlockSpec((tm, d), row),
        out_shape=jax.ShapeDtypeStruct((m, d), f32),
        compiler_params=_params(("parallel",)),
        name="out_proj_residual",
    )(*lhs_list, *([w] * n_lhs), x2, gate)


def _router_kernel(x_ref, nw_ref, sc_ref, sh_ref, rw_ref, rb_ref,
                   h_ref, eid_ref, gate_ref, rank_ref, cnt_ref, carry_ref):
    @pl.when(pl.program_id(0) == 0)
    def _():
        carry_ref[...] = jnp.zeros_like(carry_ref)

    h = _norm_mod(x_ref[...], nw_ref[...], sc_ref[0], sh_ref[0])
    h_ref[...] = h
    tm = h.shape[0]
    logits = jnp.dot(h, rw_ref[...], preferred_element_type=f32,
                     precision=lax.Precision.HIGHEST) + rb_ref[...]
    lane = lax.broadcasted_iota(i32, (tm, LANES), 1)
    work = jnp.where(lane < N_EXPERTS, logits, -jnp.inf)
    hots, vals, ids = [], [], []
    for _ in range(TOP_K):
        mx = jnp.max(work, axis=-1, keepdims=True)
        idx = jnp.min(jnp.where(work == mx, lane, LANES), axis=-1, keepdims=True)
        hot = lane == idx
        work = jnp.where(hot, -jnp.inf, work)
        hots.append(hot)
        vals.append(mx)
        ids.append(idx)
    es = [jnp.exp(v - vals[0]) for v in vals]
    denom = es[0] + es[1] + es[2] + es[3]
    member = jnp.zeros((tm, LANES), f32)
    for hot in hots:
        member = member + jnp.where(hot, 1.0, 0.0)
    r2 = lax.broadcasted_iota(i32, (tm, tm), 0)
    c2 = lax.broadcasted_iota(i32, (tm, tm), 1)
    before = jnp.where(c2 < r2, 1.0, 0.0).astype(bf16)
    prior = jnp.dot(before, member.astype(bf16), preferred_element_type=f32) + carry_ref[...]
    eid = jnp.zeros((tm, LANES), i32)
    gate = jnp.zeros((tm, LANES), f32)
    rank = jnp.zeros((tm, LANES), i32)
    for k in range(TOP_K):
        rk = jnp.sum(jnp.where(hots[k], prior, 0.0), axis=-1, keepdims=True)
        eid = jnp.where(lane == k, ids[k], eid)
        gate = jnp.where(lane == k, es[k] / denom, gate)
        rank = jnp.where(lane == k, rk.astype(i32), rank)
    eid_ref[...] = eid
    gate_ref[...] = gate
    rank_ref[...] = rank
    carry_ref[...] = carry_ref[...] + jnp.sum(member, axis=0, keepdims=True)
    cnt_ref[...] = carry_ref[...]


def _router(x2, nw, sc, sh, rw, rb, seq, tm=512):
    m, d = x2.shape
    tpb = seq // tm
    row = lambda i: (i, 0)
    per_b = lambda i: (i // tpb, 0, 0)
    fixed = lambda i: (0, 0)
    return pl.pallas_call(
        _router_kernel,
        grid=(m // tm,),
        in_specs=[pl.BlockSpec((tm, d), row),
                  pl.BlockSpec((1, d), fixed),
                  pl.BlockSpec((1, 1, d), per_b),
                  pl.BlockSpec((1, 1, d), per_b),
                  pl.BlockSpec((d, LANES), fixed),
                  pl.BlockSpec((1, LANES), fixed)],
        out_specs=[pl.BlockSpec((tm, d), row),
                   pl.BlockSpec((tm, LANES), row),
                   pl.BlockSpec((tm, LANES), row),
                   pl.BlockSpec((tm, LANES), row),
                   pl.BlockSpec((1, LANES), fixed)],
        out_shape=[jax.ShapeDtypeStruct((m, d), f32),
                   jax.ShapeDtypeStruct((m, LANES), i32),
                   jax.ShapeDtypeStruct((m, LANES), f32),
                   jax.ShapeDtypeStruct((m, LANES), i32),
                   jax.ShapeDtypeStruct((1, LANES), f32)],
        scratch_shapes=[pltpu.VMEM((1, LANES), f32)],
        compiler_params=_params(("arbitrary",)),
        name="ffn_norm_router",
    )(x2, nw, sc, sh, rw, rb)


def _dispatch_kernel(gend_ref, pos_ref, h_ref, out_ref, zero_ref, sem, zsem):
    tm = h_ref.shape[0]

    @pl.when(pl.program_id(0) == 0)
    def _():
        zero_ref[...] = jnp.zeros_like(zero_ref)

        def zcopy(e):
            r0 = pl.multiple_of(jnp.maximum(gend_ref[e] - EXPERT_TILE, 0), EXPERT_TILE)
            return pltpu.make_async_copy(zero_ref, out_ref.at[pl.ds(r0, EXPERT_TILE), :], zsem)

        def zissue(e, carry):
            zcopy(e).start()
            return carry

        def zdrain(e, carry):
            zcopy(e).wait()
            return carry

        lax.fori_loop(0, N_EXPERTS, zissue, 0)
        lax.fori_loop(0, N_EXPERTS, zdrain, 0)

    def copy(r, k):
        return pltpu.make_async_copy(h_ref.at[pl.ds(r, 1), :],
                                     out_ref.at[pl.ds(pos_ref[TOP_K * r + k], 1), :], sem)

    def issue(r, carry):
        for k in range(TOP_K):
            copy(r, k).start()
        return carry

    def drain(r, carry):
        for k in range(TOP_K):
            copy(r, k).wait()
        return carry

    lax.fori_loop(0, tm, issue, 0, unroll=4)
    lax.fori_loop(0, tm, drain, 0, unroll=4)


def _dispatch(h2, pos_flat, gend, n_rows, tm=512):
    m, d = h2.shape
    return pl.pallas_call(
        _dispatch_kernel,
        grid_spec=pltpu.PrefetchScalarGridSpec(
            num_scalar_prefetch=1,
            grid=(m // tm,),
            in_specs=[pl.BlockSpec((tm * TOP_K,), lambda i, ge: (i,), memory_space=pltpu.SMEM),
                      pl.BlockSpec((tm, d), lambda i, ge: (i, 0))],
            out_specs=pl.BlockSpec(memory_space=pl.ANY),
            scratch_shapes=[pltpu.VMEM((EXPERT_TILE, d), f32),
                            pltpu.SemaphoreType.DMA(()), pltpu.SemaphoreType.DMA(())]),
        out_shape=jax.ShapeDtypeStruct((n_rows, d), f32),
        compiler_params=_params(("arbitrary",)),
        name="moe_dispatch",
    )(gend, pos_flat, h2)


def _expert_kernel(te_ref, nu_ref, x_ref, w1_ref, b1_ref, w2_ref, b2_ref, o_ref, w1b_ref, w2b_ref):
    i = pl.program_id(0)
    e = te_ref[i]
    e_prev = te_ref[jnp.maximum(i - 1, 0)]

    @pl.when(jnp.logical_or(i == 0, e != e_prev))
    def _():
        w1b_ref[...] = w1_ref[0, 0].astype(bf16)
        w2b_ref[...] = w2_ref[0, 0].astype(bf16)

    @pl.when(i < nu_ref[0])
    def _():
        x = x_ref[...].astype(bf16)
        gu = jnp.dot(x, w1b_ref[...], preferred_element_type=f32) + b1_ref[0, 0]
        g = jnp.minimum(gu[:, :D_FF], SWIGLU_LIMIT)
        u = jnp.clip(gu[:, D_FF:], -SWIGLU_LIMIT, SWIGLU_LIMIT)
        act = g * jax.nn.sigmoid(SWIGLU_ALPHA * g) * (u + 1.0)
        o_ref[...] = jnp.dot(act.astype(bf16), w2b_ref[...], preferred_element_type=f32) + b2_ref[0, 0]

    @pl.when(i >= nu_ref[0])
    def _():
        o_ref[...] = jnp.zeros_like(o_ref)


def _expert_ffn(xs, tile_expert, n_used, w1, b1, w2, b2, layer):
    n_rows, d = xs.shape
    tm = EXPERT_TILE
    by_e = lambda i, te, nu: (layer, te[i], 0, 0)
    return pl.pallas_call(
        _expert_kernel,
        grid_spec=pltpu.PrefetchScalarGridSpec(
            num_scalar_prefetch=2,
            grid=(n_rows // tm,),
            in_specs=[pl.BlockSpec((tm, d), lambda i, te, nu: (i, 0)),
                      pl.BlockSpec((1, 1, d, 2 * D_FF), by_e),
                      pl.BlockSpec((1, 1, 1, 2 * D_FF), by_e),
                      pl.BlockSpec((1, 1, D_FF, d), by_e),
                      pl.BlockSpec((1, 1, 1, d), by_e)],
            out_specs=pl.BlockSpec((tm, d), lambda i, te, nu: (i, 0)),
            scratch_shapes=[pltpu.VMEM((d, 2 * D_FF), bf16), pltpu.VMEM((D_FF, d), bf16)]),
        out_shape=jax.ShapeDtypeStruct((n_rows, d), f32),
        compiler_params=_params(("arbitrary",)),
        name="expert_ffn",
    )(tile_expert, n_used, xs, w1, b1, w2, b2)


def _combine_kernel(pos_ref, ys_ref, gate_ref, x_ref, g2_ref, fw_ref, o_ref, buf_ref, sem, *, final_norm):
    tm = x_ref.shape[0]

    def copy(r, k):
        return pltpu.make_async_copy(ys_ref.at[pl.ds(pos_ref[TOP_K * r + k], 1), :],
                                     buf_ref.at[k, pl.ds(r, 1), :], sem)

    def issue(r, carry):
        for k in range(TOP_K):
            copy(r, k).start()
        return carry

    def drain(r, carry):
        for k in range(TOP_K):
            copy(r, k).wait()
        return carry

    lax.fori_loop(0, tm, issue, 0, unroll=4)
    lax.fori_loop(0, tm, drain, 0, unroll=4)
    gates = gate_ref[...]
    acc = gates[:, 0:1] * buf_ref[0]
    for k in range(1, TOP_K):
        acc = acc + gates[:, k:k + 1] * buf_ref[k]
    y = x_ref[...] + g2_ref[0] * acc
    if final_norm:
        y = y * lax.rsqrt(jnp.mean(y * y, axis=-1, keepdims=True) + NORM_EPS) * fw_ref[...]
    o_ref[...] = y


def _combine(ys, pos_flat, gates, x2, g2, final_w, seq, final_norm, tm=256):
    m, d = x2.shape
    tpb = seq // tm
    row = lambda i: (i, 0)
    return pl.pallas_call(
        functools.partial(_combine_kernel, final_norm=final_norm),
        grid=(m // tm,),
        in_specs=[pl.BlockSpec((tm * TOP_K,), lambda i: (i,), memory_space=pltpu.SMEM),
                  pl.BlockSpec(memory_space=pl.ANY),
                  pl.BlockSpec((tm, LANES), row),
                  pl.BlockSpec((tm, d), row),
                  pl.BlockSpec((1, 1, d), lambda i: (i // tpb, 0, 0)),
                  pl.BlockSpec((1, d), lambda i: (0, 0))],
        out_specs=pl.BlockSpec((tm, d), row),
        out_shape=jax.ShapeDtypeStruct((m, d), f32),
        scratch_shapes=[pltpu.VMEM((TOP_K, tm, d), f32), pltpu.SemaphoreType.DMA(())],
        compiler_params=_params(("arbitrary",)),
        name="moe_combine",
    )(pos_flat, ys, gates, x2, g2, final_w)


def _routed_ffn(x2, nw, sc, sh, g2, rw, rb, w1, b1, w2, b2, layer, final_w, seq, final_norm):
    m, d = x2.shape
    rw_pad = jnp.pad(rw, ((0, 0), (0, LANES - N_EXPERTS)))
    rb_pad = jnp.pad(rb, (0, LANES - N_EXPERTS)).reshape(1, LANES)
    h2, eid, gates, rank, cnt = _router(x2, nw, sc, sh, rw_pad, rb_pad, seq)
    tm = EXPERT_TILE
    n_tiles = (m * TOP_K) // tm + N_EXPERTS
    counts = cnt[0, :N_EXPERTS].astype(i32)
    padded = ((counts + tm - 1) // tm) * tm
    gend = jnp.cumsum(padded)
    gstart = gend - padded
    n_used = (gend[-1] // tm).astype(i32).reshape(1)
    tile_start = jnp.arange(n_tiles, dtype=i32) * tm
    tile_expert = jnp.sum((gend[None, :] <= tile_start[:, None]).astype(i32), axis=1)
    tile_expert = jnp.minimum(tile_expert, N_EXPERTS - 1).astype(i32)
    pos = gstart[eid[:, :TOP_K]] + rank[:, :TOP_K]
    pos_flat = pos.reshape(-1).astype(i32)
    xs = _dispatch(h2, pos_flat, gend.astype(i32), n_tiles * tm)
    depth = w1.shape[0]
    ys = _expert_ffn(xs, tile_expert, n_used, w1, b1.reshape(depth, N_EXPERTS, 1, 2 * D_FF),
                     w2, b2.reshape(depth, N_EXPERTS, 1, d), layer)
    return _combine(ys, pos_flat, gates, x2, g2, final_w, seq, final_norm)


def _dsa_prep_kernel(y_ref, qn_ref, wuq_ref, wiq_ref, lnw_ref, lnb_ref, c_ref, s1_ref, s2_ref,
                     q_ref, k_ref, v_ref, qi_ref, ki_ref, wi_ref):
    y = y_ref[...]
    c, s1, s2 = c_ref[...], s1_ref[...], s2_ref[...]
    cq = y[:, :C_Q_RANK]
    cq = (cq * lax.rsqrt(jnp.mean(cq * cq, axis=-1, keepdims=True) + NORM_EPS) * qn_ref[...]).astype(bf16)
    q = jnp.dot(cq, wuq_ref[...], preferred_element_type=f32)
    for j in range(q.shape[1] // LANES):
        sl = slice(j * LANES, (j + 1) * LANES)
        q_ref[sl, :] = (_rot(q[:, sl], c, s1, s2) * (HEAD_DIM ** -0.5)).T.astype(bf16)
    qi = jnp.dot(cq, wiq_ref[...], preferred_element_type=f32)
    for j in range(qi.shape[1] // LANES):
        sl = slice(j * LANES, (j + 1) * LANES)
        qi_ref[:, sl] = (_rot(qi[:, sl], c, s1, s2) * (IDX_DIM ** -0.5)).astype(bf16)
    k_ref[...] = y[:, C_K0:C_V0].astype(bf16)
    ones_col = jnp.where(lax.broadcasted_iota(i32, (y.shape[0], HEAD_DIM), 1) == 0, 1.0, 0.0)
    for g in range(C_KV_HEADS):
        vg = jnp.concatenate([y[:, C_V0 + g * HEAD_DIM:C_V0 + (g + 1) * HEAD_DIM], ones_col], axis=1)
        v_ref[g * LANES:(g + 1) * LANES, :] = vg.T.astype(bf16)
    t6 = y[:, C_IK0:C_IK0 + LANES]
    ik = t6[:, :IDX_DIM]
    mu = jnp.mean(ik, axis=-1, keepdims=True)
    var = jnp.mean(jnp.square(ik - mu), axis=-1, keepdims=True)
    ln = (ik - mu) * lax.rsqrt(var + NORM_EPS) * lnw_ref[...] + lnb_ref[...]
    ln = jnp.concatenate([ln, jnp.zeros_like(ln)], axis=1)
    ki_ref[...] = _rot(ln, c, s1, s2)[:, :IDX_DIM].astype(bf16)
    wi_ref[...] = t6[:, IDX_DIM:IDX_DIM + IDX_HEADS] * (IDX_HEADS ** -0.5)


def _dsa_prep(y, qn, wuq, wiq, lnw, lnb, tabs, tm=256):
    m = y.shape[0]
    row = lambda i: (i, 0)
    fixed = lambda i: (0, 0)
    kvw = C_KV_HEADS * HEAD_DIM
    return pl.pallas_call(
        _dsa_prep_kernel,
        grid=(m // tm,),
        in_specs=[pl.BlockSpec((tm, C_IN_PAD), row),
                  pl.BlockSpec((1, C_Q_RANK), fixed),
                  pl.BlockSpec(wuq.shape, fixed),
                  pl.BlockSpec(wiq.shape, fixed),
                  pl.BlockSpec((1, IDX_DIM), fixed),
                  pl.BlockSpec((1, IDX_DIM), fixed),
                  pl.BlockSpec((tm, LANES), row),
                  pl.BlockSpec((tm, LANES), row),
                  pl.BlockSpec((tm, LANES), row)],
        out_specs=[pl.BlockSpec((C_HEADS * HEAD_DIM, tm), lambda i: (0, i)),
                   pl.BlockSpec((tm, kvw), row),
                   pl.BlockSpec((C_KV_HEADS * LANES, tm), lambda i: (0, i)),
                   pl.BlockSpec((tm, IDX_HEADS * IDX_DIM), row),
                   pl.BlockSpec((tm, IDX_DIM), row),
                   pl.BlockSpec((tm, IDX_HEADS), row)],
        out_shape=[jax.ShapeDtypeStruct((C_HEADS * HEAD_DIM, m), bf16),
                   jax.ShapeDtypeStruct((m, kvw), bf16),
                   jax.ShapeDtypeStruct((C_KV_HEADS * LANES, m), bf16),
                   jax.ShapeDtypeStruct((m, IDX_HEADS * IDX_DIM), bf16),
                   jax.ShapeDtypeStruct((m, IDX_DIM), bf16),
                   jax.ShapeDtypeStruct((m, IDX_HEADS), f32)],
        compiler_params=_params(("parallel",)),
        name="dsa_prep",
    )(y, qn, wuq, wiq, lnw, lnb, *tabs)


def _dsa_select_kernel(qi_ref, wi_ref, ki_ref, mask_ref, key_ref, *, seq, n_sel):
    i = pl.program_id(1)
    qb, kc = C_QBLOCK, KEY_CHUNK
    nkc = (i * qb + qb + kc - 1) // kc
    qs = jnp.concatenate([qi_ref[:, h * IDX_DIM:(h + 1) * IDX_DIM] for h in range(IDX_HEADS)], axis=0)
    wi = wi_ref[...]
    qpos = i * qb + lax.broadcasted_iota(i32, (qb, kc), 0)
    kiota = lax.broadcasted_iota(i32, (qb, kc), 1)

    def score_chunk(c, carry):
        k0 = pl.multiple_of(c * kc, kc)
        lg = lax.dot_general(qs, ki_ref[pl.ds(k0, kc), :], _NT, preferred_element_type=f32)
        sc = wi[:, 0:1] * jnp.maximum(lg[0:qb], 0.0)
        for h in range(1, IDX_HEADS):
            sc = sc + wi[:, h:h + 1] * jnp.maximum(lg[h * qb:(h + 1) * qb], 0.0)
        sc = jnp.where(sc == 0.0, 0.0, sc)
        sc = jnp.where(k0 + kiota <= qpos, sc, -jnp.inf)
        bits = pltpu.bitcast(sc, i32)
        key_ref[:, pl.ds(k0, kc)] = bits ^ ((bits >> 31) & 0x7FFFFFFF)
        return carry

    lax.fori_loop(0, nkc, score_chunk, 0)

    def count(pred):
        def body(c, acc):
            k0 = pl.multiple_of(c * kc, kc)
            hit = jnp.where(pred(key_ref[:, pl.ds(k0, kc)]), 1.0, 0.0)
            for j in range(kc // LANES):
                acc = acc + hit[:, j * LANES:(j + 1) * LANES]
            return acc
        acc = lax.fori_loop(0, nkc, body, jnp.zeros((qb, LANES), f32))
        return jnp.sum(acc, axis=-1, keepdims=True)

    thr = jnp.where(count(lambda k: k >= 0) >= n_sel, 0, INT_MIN).astype(i32)

    def bit_step(bi, thr):
        cand = thr | jnp.left_shift(jnp.int32(1), 30 - bi)
        return jnp.where(count(lambda k: k >= cand) >= n_sel, cand, thr)

    thr = lax.fori_loop(0, 31, bit_step, thr)
    allowed = (n_sel - count(lambda k: k > thr)).astype(f32)
    r2 = lax.broadcasted_iota(i32, (kc, kc), 0)
    c2 = lax.broadcasted_iota(i32, (kc, kc), 1)
    upto = jnp.where(r2 <= c2, 1.0, 0.0).astype(bf16)

    def out_chunk(c, ties_before):
        k0 = pl.multiple_of(c * kc, kc)
        key = key_ref[:, pl.ds(k0, kc)]
        eq = jnp.where(key == thr, 1.0, 0.0)
        tie_rank = jnp.dot(eq.astype(bf16), upto, preferred_element_type=f32) + ties_before
        tie_ok = (key == thr) & (tie_rank <= allowed)
        sel = (k0 + kiota <= qpos) & ((key > thr) | tie_ok)
        mask_ref[0, pl.ds(k0, kc), :] = jnp.where(sel, 1.0, 0.0).T.astype(i32).astype(jnp.int8)
        return ties_before + jnp.sum(eq, axis=-1, keepdims=True)

    lax.fori_loop(0, nkc, out_chunk, jnp.zeros((qb, 1), f32))

    def zero_chunk(c, carry):
        k0 = pl.multiple_of(c * kc, kc)
        mask_ref[0, pl.ds(k0, kc), :] = jnp.zeros((kc, qb), jnp.int8)
        return carry

    lax.fori_loop(nkc, seq // kc, zero_chunk, 0)


def _dsa_select(qi, wi, ki, batch, seq):
    nq = seq // C_QBLOCK
    n_sel = min(IDX_TOPK, seq // 4)
    return pl.pallas_call(
        functools.partial(_dsa_select_kernel, seq=seq, n_sel=n_sel),
        grid=(batch, nq),
        in_specs=[pl.BlockSpec((C_QBLOCK, IDX_HEADS * IDX_DIM), lambda b, i: (b * nq + i, 0)),
                  pl.BlockSpec((C_QBLOCK, IDX_HEADS), lambda b, i: (b * nq + i, 0)),
                  pl.BlockSpec((seq, IDX_DIM), lambda b, i: (b, 0))],
        out_specs=pl.BlockSpec((1, seq, C_QBLOCK), lambda b, i: (b, 0, i)),
        out_shape=jax.ShapeDtypeStruct((batch, seq, seq), jnp.int8),
        scratch_shapes=[pltpu.VMEM((C_QBLOCK, seq), i32)],
        compiler_params=_params(("parallel", "parallel")),
        name="dsa_select",
    )(qi, wi, ki)


def _dsa_attn_kernel(qt_ref, k_ref, vt_ref, mask_ref, o_ref, qx_ref, m_ref, acc_ref):
    i = pl.program_id(1)
    qb, kc = C_QBLOCK, ATTN_CHUNK
    g_sz = C_HEADS // C_KV_HEADS
    nkc = (i * qb + qb + kc - 1) // kc
    qx_ref[...] = jnp.zeros_like(qx_ref)
    for g in range(C_KV_HEADS):
        for hl in range(g_sz):
            h = g * g_sz + hl
            qx_ref[g, g * HEAD_DIM:(g + 1) * HEAD_DIM, hl * qb:(hl + 1) * qb] = qt_ref[h * HEAD_DIM:(h + 1) * HEAD_DIM, :]
    m_ref[...] = jnp.full_like(m_ref, NEG_BIG)
    acc_ref[...] = jnp.zeros_like(acc_ref)

    def one_chunk(k0):
        kk = k_ref[pl.ds(k0, kc), :]
        bias = jnp.where(mask_ref[0, pl.ds(k0, kc), :].astype(i32) != 0, 0.0, -jnp.inf)
        bias = jnp.concatenate([bias] * g_sz, axis=1)
        for g in range(C_KV_HEADS):
            s = jnp.dot(kk, qx_ref[g], preferred_element_type=f32) + bias
            m_old = m_ref[g]
            m_new = jnp.maximum(m_old, jnp.max(s, axis=0, keepdims=True))
            p = jnp.exp(s - m_new).astype(bf16)
            acc_ref[g] = (jnp.exp(m_old - m_new) * acc_ref[g]
                          + jnp.dot(vt_ref[g * LANES:(g + 1) * LANES, pl.ds(k0, kc)], p, preferred_element_type=f32))
            m_ref[g] = m_new

    def body(c, carry):
        one_chunk(pl.multiple_of(c * kc, kc))
        return carry

    lax.fori_loop(0, nkc, body, 0)
    outs = []
    for g in range(C_KV_HEADS):
        a = acc_ref[g]
        o = a[:HEAD_DIM] / a[HEAD_DIM:HEAD_DIM + 1]
        for hl in range(g_sz):
            outs.append(o[:, hl * qb:(hl + 1) * qb].T)
    o_ref[...] = jnp.concatenate(outs, axis=1).astype(o_ref.dtype)


def _dsa_attention(qt, k, vt, mask, batch, seq):
    nq = seq // C_QBLOCK
    g_cols = (C_HEADS // C_KV_HEADS) * C_QBLOCK
    kvw = C_KV_HEADS * HEAD_DIM
    return pl.pallas_call(
        _dsa_attn_kernel,
        grid=(batch, nq),
        in_specs=[pl.BlockSpec((C_HEADS * HEAD_DIM, C_QBLOCK), lambda b, i: (0, b * nq + i)),
                  pl.BlockSpec((seq, kvw), lambda b, i: (b, 0)),
                  pl.BlockSpec((C_KV_HEADS * LANES, seq), lambda b, i: (0, b)),
                  pl.BlockSpec((1, seq, C_QBLOCK), lambda b, i: (b, 0, i))],
        out_specs=pl.BlockSpec((C_QBLOCK, C_HEADS * HEAD_DIM), lambda b, i: (b * nq + i, 0)),
        out_shape=jax.ShapeDtypeStruct((batch * seq, C_HEADS * HEAD_DIM), bf16),
        scratch_shapes=[pltpu.VMEM((C_KV_HEADS, kvw, g_cols), bf16),
                        pltpu.VMEM((C_KV_HEADS, 1, g_cols), f32),
                        pltpu.VMEM((C_KV_HEADS, LANES, g_cols), f32)],
        compiler_params=_params(("parallel", "parallel")),
        name="dsa_attention",
    )(qt, k, vt, mask)


def _rotary_tables(positions, rot_dim):
    half = rot_dim // 2
    inv_freq = ROPE_THETA ** (-jnp.arange(0, rot_dim, 2, dtype=f32) / rot_dim)
    ang = positions.astype(f32).reshape(-1, 1) * inv_freq
    cos, sin = jnp.cos(ang), jnp.sin(ang)
    n = ang.shape[0]
    rest = HEAD_DIM - 2 * half
    c = jnp.concatenate([cos, cos, jnp.ones((n, rest), f32)], axis=-1)
    s1 = jnp.concatenate([-sin, jnp.zeros((n, half + rest), f32)], axis=-1)
    s2 = jnp.concatenate([jnp.zeros((n, half), f32), sin, jnp.zeros((n, rest), f32)], axis=-1)
    rep = LANES // HEAD_DIM
    return tuple(jnp.tile(t, (1, rep)) for t in (c, s1, s2))


def kernel(x, c, positions, mod_w, mod_b, norm_mix_w, norm_ffn_w, ab_w_in, ab_w_out, a_sinks, b_lb_logits,
           b_onorm_w, c_w_in, c_q_norm_w, c_w_uq, c_w_iq, c_ik_norm_w, c_ik_norm_b, c_w_out, router_w,
           router_b, moe_w1, moe_b1, moe_w2, moe_b2, final_norm_w):
    batch, seq, d = x.shape
    depth = mod_w.shape[0]
    tabs = _rotary_tables(positions, ROT_DIM)
    tabs_idx = _rotary_tables(positions, IDX_ROT_DIM)
    lower_bounds = jnp.cumsum(jax.nn.softmax(b_lb_logits.astype(f32), axis=0), axis=0)
    cond = jnp.pad(jax.nn.silu(c), ((0, 8 - batch), (0, 0)))
    x2 = x.reshape(batch * seq, d)
    final_w = final_norm_w.reshape(1, d)
    for layer in range(depth):
        mod = _dense(cond, mod_w[layer], mod_b[layer].reshape(1, -1))[:batch]
        sh1, sc1, g1, sh2, sc2, g2 = [t.reshape(batch, 1, d) for t in jnp.split(mod, 6, axis=-1)]
        nw = norm_mix_w[layer].reshape(1, d)
        j = layer // 2
        if layer % 2 == 0:
            proj = _norm_mod_matmul(x2, nw, sc1, sh1, ab_w_in[j].astype(bf16), tabs, seq,
                                    rot_lo=AB_Q0 // LANES, rot_hi=AB_V0 // LANES)
            out_a = _swa_attention(proj, a_sinks[j], batch, seq)
            out_b = _hgrn2(proj, lower_bounds[j].reshape(B_HEADS, 1, B_KEY_DIM),
                           b_onorm_w[j].reshape(1, B_VAL_DIM), batch, seq)
            x2 = _proj_residual([out_a, out_b], ab_w_out[j].astype(bf16), x2, g1, seq)
        else:
            w_in = jnp.pad(c_w_in[j], ((0, 0), (0, C_IN_PAD - c_w_in.shape[-1]))).astype(bf16)
            y = _norm_mod_matmul(x2, nw, sc1, sh1, w_in, tabs, seq,
                                 rot_lo=C_K0 // LANES, rot_hi=C_V0 // LANES)
            q, k, v, qi, ki, wi = _dsa_prep(y, c_q_norm_w[j].reshape(1, -1), c_w_uq[j].astype(bf16),
                                            c_w_iq[j].astype(bf16), c_ik_norm_w[j].reshape(1, -1),
                                            c_ik_norm_b[j].reshape(1, -1), tabs_idx)
            mask = _dsa_select(qi, wi, ki, batch, seq)
            o = _dsa_attention(q, k, v, mask, batch, seq)
            x2 = _proj_residual([o], c_w_out[j].astype(bf16), x2, g1, seq)
        x2 = _routed_ffn(x2, norm_ffn_w[layer].reshape(1, d), sc2, sh2, g2, router_w[layer], router_b[layer],
                         moe_w1, moe_b1, moe_w2, moe_b2, layer, final_w, seq,
                         final_norm=(layer == depth - 1))
    return x2.reshape(batch, seq, d)
```
